```python
import jax, jax.numpy as jnp
from jax import lax
import numpy as np

D_MODEL = 1024
BATCH = 16
SEQ = 2048
DEPTH = 2

MLA_HEADS = 8
QK_NOPE = 64
QK_ROPE = 32
V_HEAD = 64
Q_LORA = 256
KV_LORA = 256
ROPE_THETA = 10000.0
ATTN_BLOCK = 128
SC_GROUPS = 8
SC_WIDTH = 512
SC_K = 3
CF_GROUPS = 8
CF_WIDTH = 512
CF_K = 31
GM_GROUPS = 4
GM_WIDTH = 512
GM_CHUNK = 128
N_BRANCH = 4
FFN_HIDDEN = -(-8 * D_MODEL // (3 * 256)) * 256
N_IN = Q_LORA + KV_LORA + QK_ROPE + 3 * SC_WIDTH + 2 * CF_WIDTH + 2 * GM_WIDTH + N_BRANCH * D_MODEL
EPS = 1e-6

kernel_name = "hybrid_gated_parallel_encoder"


def rms_norm(x, g):
    x32 = x.astype(jnp.float32)
    y = x32 * lax.rsqrt(jnp.mean(x32 * x32, axis=-1, keepdims=True) + EPS)
    return (y * g.astype(jnp.float32)).astype(x.dtype)


def layer_norm(x, g, b):
    x32 = x.astype(jnp.float32)
    mu = jnp.mean(x32, axis=-1, keepdims=True)
    xc = x32 - mu
    var = jnp.mean(xc * xc, axis=-1, keepdims=True)
    y = xc * lax.rsqrt(var + EPS) * g.astype(jnp.float32) + b.astype(jnp.float32)
    return y.astype(x.dtype)


def apply_rope(t, cos, sin):
    cos = cos.astype(t.dtype)
    sin = sin.astype(t.dtype)
    t1, t2 = jnp.split(t, 2, axis=-1)
    return jnp.concatenate([t1 * cos - t2 * sin, t2 * cos + t1 * sin], axis=-1)


def split_points():
    sizes = [Q_LORA, KV_LORA, QK_ROPE, SC_WIDTH, SC_WIDTH, SC_WIDTH,
             CF_WIDTH, CF_WIDTH, GM_WIDTH, GM_WIDTH]
    pts, acc = [], 0
    for s in sizes:
        acc += s
        pts.append(acc)
    return pts


def mla_branch(c_q, c_kv, k_rope, cos, sin, q_norm, w_uq, kv_norm, w_ukv):
    B, S, _ = c_q.shape
    q = (rms_norm(c_q, q_norm) @ w_uq).reshape(B, S, MLA_HEADS, QK_NOPE + QK_ROPE)
    q_nope = q[..., :QK_NOPE]
    q_rope = apply_rope(q[..., QK_NOPE:], cos[:, :, None, :], sin[:, :, None, :])
    kv = (rms_norm(c_kv, kv_norm) @ w_ukv).reshape(B, S, MLA_HEADS, QK_NOPE + V_HEAD)
    k_nope = kv[..., :QK_NOPE]
    v = kv[..., QK_NOPE:]
    k_rope = apply_rope(k_rope, cos, sin)
    scale = (QK_NOPE + QK_ROPE) ** -0.5
    nb = S // ATTN_BLOCK

    def blocks(t):
        return jnp.moveaxis(t.reshape(B, nb, ATTN_BLOCK, *t.shape[2:]), 1, 0)

    def attend(qs):
        qn, qr = qs
        s = (jnp.einsum('bqhd,bkhd->bhqk', qn, k_nope)
             + jnp.einsum('bqhr,bkr->bhqk', qr, k_rope))
        p = jax.nn.softmax(s.astype(jnp.float32) * scale, axis=-1).astype(v.dtype)
        return jnp.einsum('bhqk,bkhd->bqhd', p, v)

    o = lax.map(attend, (blocks(q_nope), blocks(q_rope)))
    return jnp.moveaxis(o, 0, 1).reshape(B, S, MLA_HEADS * V_HEAD)


def short_conv_branch(b_gate, c_gate, x_in, w):
    S = x_in.shape[1]
    z = c_gate * x_in
    pad = SC_K // 2
    zp = jnp.pad(z, ((0, 0), (pad, pad), (0, 0)))
    w = w.astype(z.dtype)
    y = zp[:, 0:S] * w[0]
    for k in range(1, SC_K):
        y = y + zp[:, k:k + S] * w[k]
    return b_gate * y


def conformer_branch(a, gate, conv_w, conv_b, ln_g, ln_b):
    z = a * jax.nn.sigmoid(gate)
    y = lax.conv_general_dilated(
        z, conv_w[:, None, :].astype(z.dtype), window_strides=(1,),
        padding=[(CF_K // 2, CF_K // 2)], dimension_numbers=('NWC', 'WIO', 'NWC'),
        feature_group_count=CF_WIDTH) + conv_b
    return jax.nn.silu(layer_norm(y, ln_g, ln_b))


def gmlp_branch(u, v, ln_g, ln_b, ws, bs):
    B, S, _ = u.shape
    nc = S // GM_CHUNK
    v = layer_norm(v, ln_g, ln_b).reshape(B, nc, GM_CHUNK, GM_GROUPS, GM_WIDTH // GM_GROUPS)
    mixed = jnp.einsum('gts,bnsgc->bntgc', ws, v) + bs.T[None, None, :, :, None]
    return u * mixed.reshape(B, S, GM_WIDTH)


def setup_inputs(seed: int = 0) -> dict:
    key = jax.random.key(seed)
    ks = iter(jax.random.split(key, 40))

    def nrm(shape, scale):
        return jax.random.normal(next(ks), shape, jnp.float32) * scale

    def gain(shape):
        return 1.0 + nrm(shape, 0.05)

    L, D = DEPTH, D_MODEL
    x = jax.random.normal(next(ks), (BATCH, SEQ, D), jnp.float32)
    offs = jax.random.randint(next(ks), (BATCH, 1), 0, SEQ, dtype=jnp.int32)
    positions = jnp.arange(SEQ, dtype=jnp.int32)[None, :] + offs
    return {
        "x": x,
        "positions": positions,
        "norm_mix_pre": gain((L, D)),
        "w_in": nrm((L, D, N_IN), D ** -0.5),
        "mla_q_norm": gain((L, Q_LORA)),
        "w_uq": nrm((L, Q_LORA, MLA_HEADS * (QK_NOPE + QK_ROPE)), Q_LORA ** -0.5),
        "mla_kv_norm": gain((L, KV_LORA)),
        "w_ukv": nrm((L, KV_LORA, MLA_HEADS * (QK_NOPE + V_HEAD)), KV_LORA ** -0.5),
        "w_o_mla": nrm((L, MLA_HEADS * V_HEAD, D), (MLA_HEADS * V_HEAD) ** -0.5),
        "sc_conv_w": nrm((L, SC_K, SC_WIDTH), SC_K ** -0.5),
        "w_o_sc": nrm((L, SC_WIDTH, D), SC_WIDTH ** -0.5),
        "cf_conv_w": nrm((L, CF_K, CF_WIDTH), CF_K ** -0.5),
        "cf_conv_b": nrm((L, CF_WIDTH), 0.02),
        "cf_ln_g": gain((L, CF_WIDTH)),
        "cf_ln_b": nrm((L, CF_WIDTH), 0.02),
        "w_o_cf": nrm((L, CF_WIDTH, D), CF_WIDTH ** -0.5),
        "gm_ln_g": gain((L, GM_WIDTH)),
        "gm_ln_b": nrm((L, GM_WIDTH), 0.02),
        "gm_ws": nrm((L, GM_GROUPS, GM_CHUNK, GM_CHUNK), GM_CHUNK ** -0.5),
        "gm_bs": 1.0 + nrm((L, GM_GROUPS, GM_CHUNK), 0.02),
        "w_o_gm": nrm((L, GM_WIDTH, D), GM_WIDTH ** -0.5),
        "gate_b": nrm((L, N_BRANCH, D), 0.02),
        "w_out": nrm((L, D, D), D ** -0.5),
        "norm_mix_post": gain((L, D)),
        "norm_ffn_pre": gain((L, D)),
        "w_ffn_in": nrm((L, D, 2 * FFN_HIDDEN), D ** -0.5),
        "w_ffn_out": nrm((L, FFN_HIDDEN, D), FFN_HIDDEN ** -0.5),
        "norm_ffn_post": gain((L, D)),
    }


def reference(x, positions, norm_mix_pre, w_in, mla_q_norm, w_uq, mla_kv_norm, w_ukv, w_o_mla,
              sc_conv_w, w_o_sc, cf_conv_w, cf_conv_b, cf_ln_g, cf_ln_b, w_o_cf,
              gm_ln_g, gm_ln_b, gm_ws, gm_bs, w_o_gm, gate_b, w_out, norm_mix_post,
              norm_ffn_pre, w_ffn_in, w_ffn_out, norm_ffn_post):
    B, S, D = x.shape
    inv_freq = ROPE_THETA ** (-jnp.arange(0, QK_ROPE, 2, dtype=jnp.float32) / QK_ROPE)
    ang = positions.astype(jnp.float32)[..., None] * inv_freq
    cos, sin = jnp.cos(ang), jnp.sin(ang)
    pts = split_points()

    for l in range(DEPTH):
        h = rms_norm(x, norm_mix_pre[l])
        proj = h @ w_in[l]
        (c_q, c_kv, k_rope, sc_b, sc_c, sc_x, cf_a, cf_g, gm_u, gm_v, gate_logits) = \
            jnp.split(proj, pts, axis=-1)

        y_a = mla_branch(c_q, c_kv, k_rope, cos, sin, mla_q_norm[l], w_uq[l],
                         mla_kv_norm[l], w_ukv[l]) @ w_o_mla[l]
        y_b = short_conv_branch(sc_b, sc_c, sc_x, sc_conv_w[l]) @ w_o_sc[l]
        y_c = conformer_branch(cf_a, cf_g, cf_conv_w[l], cf_conv_b[l],
                               cf_ln_g[l], cf_ln_b[l]) @ w_o_cf[l]
        y_d = gmlp_branch(jax.nn.gelu(gm_u), jax.nn.gelu(gm_v), gm_ln_g[l], gm_ln_b[l],
                          gm_ws[l], gm_bs[l]) @ w_o_gm[l]

        g = jax.nn.sigmoid(gate_logits.reshape(B, S, N_BRANCH, D) + gate_b[l])
        merged = g[:, :, 0] * y_a + g[:, :, 1] * y_b + g[:, :, 2] * y_c + g[:, :, 3] * y_d
        x = x + rms_norm(merged @ w_out[l], norm_mix_post[l])

        h2 = rms_norm(x, norm_ffn_pre[l])
        gu = h2 @ w_ffn_in[l]
        f_g, f_u = jnp.split(gu, 2, axis=-1)
        f = (jax.nn.silu(f_g) * f_u) @ w_ffn_out[l]
        x = x + rms_norm(f, norm_ffn_post[l])
    return x
```

```python
import functools

import jax
import jax.numpy as jnp
import numpy as np
from jax import lax
from jax.experimental import pallas as pl
from jax.experimental.pallas import tpu as pltpu

F32 = jnp.float32
BF16 = jnp.bfloat16

D_MODEL = 1024
MLA_HEADS = 8
QK_NOPE = 64
QK_ROPE = 32
V_HEAD = 64
Q_LORA = 256
KV_LORA = 256
ROPE_THETA = 10000.0
SC_WIDTH = 512
SC_K = 3
CF_WIDTH = 512
CF_K = 31
GM_GROUPS = 4
GM_WIDTH = 512
GM_CHUNK = 128
N_BRANCH = 4
FFN_HIDDEN = 2816
EPS = 1e-6

LANES = 128
HEAD_PAD = LANES
HALO = 16
CONV_ROWS = 16

COL_CF_A = 0
COL_CF_G = 512
COL_SC_C = 1024
COL_SC_X = 1536
COL_SC_B = 2048
COL_GM_U = 2560
COL_GM_V = 3072
COL_GATE = 3584
N_MAIN = COL_GATE + N_BRANCH * D_MODEL
HALO_COLS = COL_SC_B
N_LATENT = 640

TM_INPROJ = 1024
TN_INPROJ = 1536
TM_PREP = 512
TQ_ATTN = 512
TM_MIX = 256
TM_FFN = 512
FFN_CHUNK = 1408
VMEM_LIMIT = 56 * 1024 * 1024


def _rms(x, g):
    ms = jnp.mean(x * x, axis=-1, keepdims=True)
    return x * lax.rsqrt(ms + EPS) * g


def _layer_norm(x, g, b):
    mu = jnp.mean(x, axis=-1, keepdims=True)
    xc = x - mu
    var = jnp.mean(xc * xc, axis=-1, keepdims=True)
    return xc * lax.rsqrt(var + EPS) * g + b


def _dot(a, b):
    return jnp.dot(a, b, preferred_element_type=F32)


def _resident(shape):
    nd = len(shape)
    return pl.BlockSpec(shape, lambda *_: (0,) * nd, pipeline_mode=pl.Buffered(1))


def _params(*sem):
    return pltpu.CompilerParams(dimension_semantics=sem, vmem_limit_bytes=VMEM_LIMIT)


def _inproj_kernel(x_ref, g_ref, w_ref, o_ref, h_ref):
    @pl.when(pl.program_id(1) == 0)
    def _():
        h_ref[...] = _rms(x_ref[...], g_ref[...]).astype(BF16)

    o_ref[...] = _dot(h_ref[...], w_ref[...]).astype(o_ref.dtype)


def _inproj(x2, g, w_main):
    m = x2.shape[0]
    return pl.pallas_call(
        _inproj_kernel,
        grid=(m // TM_INPROJ, N_MAIN // TN_INPROJ),
        in_specs=[
            pl.BlockSpec((TM_INPROJ, D_MODEL), lambda i, j: (i, 0)),
            pl.BlockSpec((1, D_MODEL), lambda i, j: (0, 0)),
            pl.BlockSpec((D_MODEL, TN_INPROJ), lambda i, j: (0, j)),
        ],
        out_specs=pl.BlockSpec((TM_INPROJ, TN_INPROJ), lambda i, j: (i, j)),
        out_shape=jax.ShapeDtypeStruct((m, N_MAIN), BF16),
        scratch_shapes=[pltpu.VMEM((TM_INPROJ, D_MODEL), BF16)],
        compiler_params=_params("parallel", "arbitrary"),
    )(x2, g, w_main)


def _prep_kernel(x_ref, g_ref, wc_ref, qn_ref, kvn_ref, wqa_ref, wqb_ref, wka_ref, wkr_ref,
                 wkb_ref, wv_ref, vone_ref, cos_ref, sin_ref, q_ref, k_ref, v_ref):
    h = _rms(x_ref[...], g_ref[...]).astype(BF16)
    c = _dot(h, wc_ref[...])
    cq = _rms(c[:, :Q_LORA], qn_ref[...]).astype(BF16)
    ckv = _rms(c[:, Q_LORA:Q_LORA + KV_LORA], kvn_ref[...]).astype(BF16)
    kr = c[:, Q_LORA + KV_LORA:].astype(BF16)
    cos = cos_ref[...]
    sin = sin_ref[...]
    scale = (QK_NOPE + QK_ROPE) ** -0.5
    qa = _dot(cq, wqa_ref[...])
    qb = _dot(cq, wqb_ref[...])
    ka = _dot(ckv, wka_ref[...]) + _dot(kr, wkr_ref[...])
    kb = _dot(kr, wkb_ref[...])
    for hd in range(MLA_HEADS):
        sl = slice(hd * HEAD_PAD, (hd + 1) * HEAD_PAD)
        q_ref[:, sl] = ((qa[:, sl] * cos + qb[:, sl] * sin) * scale).astype(BF16)
        k_ref[:, sl] = (ka[:, sl] * cos + kb[:, sl] * sin).astype(BF16)
    v_ref[...] = (_dot(ckv, wv_ref[...]) + vone_ref[...]).astype(BF16)


def _prep(x2, g, lw, cos_t, sin_t):
    m = x2.shape[0]
    hp = MLA_HEADS * HEAD_PAD
    row = lambda i: (i, 0)
    slab = jax.ShapeDtypeStruct((m, hp), BF16)
    return pl.pallas_call(
        _prep_kernel,
        grid=(m // TM_PREP,),
        in_specs=[
            pl.BlockSpec((TM_PREP, D_MODEL), row),
            _resident((1, D_MODEL)),
            _resident((D_MODEL, N_LATENT)),
            _resident((1, Q_LORA)),
            _resident((1, KV_LORA)),
            _resident((Q_LORA, hp)),
            _resident((Q_LORA, hp)),
            _resident((KV_LORA, hp)),
            _resident((LANES, hp)),
            _resident((LANES, hp)),
            _resident((KV_LORA, hp)),
            _resident((1, hp)),
            pl.BlockSpec((TM_PREP, LANES), row),
            pl.BlockSpec((TM_PREP, LANES), row),
        ],
        out_specs=[pl.BlockSpec((TM_PREP, hp), row)] * 3,
        out_shape=[slab, slab, slab],
        compiler_params=_params("parallel"),
    )(x2, g, lw["w_c"], lw["q_norm"], lw["kv_norm"], lw["wq_a"], lw["wq_b"], lw["wk_a"],
      lw["wk_r"], lw["wk_b"], lw["wv"], lw["v_one"], cos_t, sin_t)


def _attn_kernel(q_ref, k_ref, v_ref, o_ref):
    outs = []
    for hh in range(2):
        sl = slice(hh * HEAD_PAD, (hh + 1) * HEAD_PAD)
        s = lax.dot_general(q_ref[:, sl], k_ref[:, sl], (((1,), (1,)), ((), ())),
                            preferred_element_type=F32)
        p = jnp.exp(s - jnp.max(s, axis=-1, keepdims=True)).astype(BF16)
        acc = _dot(p, v_ref[:, sl])
        outs.append(acc * (1.0 / acc[:, V_HEAD:V_HEAD + 1]))
    lane = lax.broadcasted_iota(jnp.int32, outs[0].shape, 1)
    o_ref[...] = jnp.where(lane < V_HEAD, outs[0], pltpu.roll(outs[1], V_HEAD, 1)).astype(BF16)


def _attention(q, k, v, batch, seq):
    m = q.shape[0]
    nq = seq // TQ_ATTN
    pair = 2 * HEAD_PAD
    return pl.pallas_call(
        _attn_kernel,
        grid=(batch, MLA_HEADS // 2, nq),
        in_specs=[
            pl.BlockSpec((TQ_ATTN, pair), lambda b, p, i: (b * nq + i, p)),
            pl.BlockSpec((seq, pair), lambda b, p, i: (b, p)),
            pl.BlockSpec((seq, pair), lambda b, p, i: (b, p)),
        ],
        out_specs=pl.BlockSpec((TQ_ATTN, 2 * V_HEAD), lambda b, p, i: (b * nq + i, p)),
        out_shape=jax.ShapeDtypeStruct((m, MLA_HEADS * V_HEAD), BF16),
        compiler_params=_params("parallel", "parallel", "arbitrary"),
    )(q, k, v)


def _mix_kernel(pm_ref, hp_ref, hn_ref, att_ref, x_ref, woa_ref, wob_ref, woc_ref, wod_ref,
                wout_ref, scw_ref, cfw_ref, cfb_ref, cflg_ref, cflb_ref, gmlg_ref, gmlb_ref,
                gmws_ref, gmbs_ref, gateb_ref, npost_ref, out_ref,
                zc_ref, zs_ref, yb_ref, yc_ref, yd_ref):
    tm = TM_MIX
    i = pl.program_id(1)
    has_prev = (i > 0).astype(F32)
    has_next = (i < pl.num_programs(1) - 1).astype(F32)

    def glu(ref, rows):
        a = ref[rows, COL_CF_A:COL_CF_A + CF_WIDTH].astype(F32)
        g = ref[rows, COL_CF_G:COL_CF_G + CF_WIDTH].astype(F32)
        return a * jax.nn.sigmoid(g)

    def scz(ref, rows):
        c = ref[rows, COL_SC_C:COL_SC_C + SC_WIDTH].astype(F32)
        x = ref[rows, COL_SC_X:COL_SC_X + SC_WIDTH].astype(F32)
        return c * x

    every = slice(None)
    zc_ref[0:HALO, :] = glu(hp_ref, every) * has_prev
    zc_ref[HALO:HALO + tm, :] = glu(pm_ref, every)
    zc_ref[HALO + tm:, :] = glu(hn_ref, every) * has_next
    zs_ref[0:HALO, :] = scz(hp_ref, every) * has_prev
    zs_ref[HALO:HALO + tm, :] = scz(pm_ref, every)
    zs_ref[HALO + tm:, :] = scz(hn_ref, every) * has_next

    cfw = cfw_ref[...]
    scw = scw_ref[...]
    cfb = cfb_ref[...]
    cflg = cflg_ref[...]
    cflb = cflb_ref[...]

    def conv_step(c, carry):
        r0 = pl.multiple_of(c * CONV_ROWS, CONV_ROWS)
        win = zc_ref[pl.ds(r0, CONV_ROWS + 2 * HALO), :]
        acc = jnp.zeros((CONV_ROWS, CF_WIDTH), F32) + cfb
        for kk in range(CF_K):
            off = HALO - CF_K // 2 + kk
            acc = acc + win[off:off + CONV_ROWS, :] * cfw[kk:kk + 1, :]
        y = _layer_norm(acc, cflg, cflb)
        yc_ref[pl.ds(r0, CONV_ROWS), :] = (y * jax.nn.sigmoid(y)).astype(BF16)

        win2 = zs_ref[pl.ds(r0, CONV_ROWS + 2 * HALO), :]
        acc2 = win2[HALO - 1:HALO - 1 + CONV_ROWS, :] * scw[0:1, :]
        for kk in range(1, SC_K):
            off = HALO - SC_K // 2 + kk
            acc2 = acc2 + win2[off:off + CONV_ROWS, :] * scw[kk:kk + 1, :]
        bg = pm_ref[pl.ds(r0, CONV_ROWS), COL_SC_B:COL_SC_B + SC_WIDTH].astype(F32)
        yb_ref[pl.ds(r0, CONV_ROWS), :] = (bg * acc2).astype(BF16)
        return carry

    lax.fori_loop(0, tm // CONV_ROWS, conv_step, 0)

    u = jax.nn.gelu(pm_ref[:, COL_GM_U:COL_GM_U + GM_WIDTH].astype(F32))
    v = jax.nn.gelu(pm_ref[:, COL_GM_V:COL_GM_V + GM_WIDTH].astype(F32))
    v = _layer_norm(v, gmlg_ref[...], gmlb_ref[...]).astype(BF16)
    gw = GM_WIDTH // GM_GROUPS
    for n in range(tm // GM_CHUNK):
        rs = slice(n * GM_CHUNK, (n + 1) * GM_CHUNK)
        for g in range(GM_GROUPS):
            cs = slice(g * gw, (g + 1) * gw)
            mixed = _dot(gmws_ref[g], v[rs, cs]) + gmbs_ref[:, g:g + 1]
            yd_ref[rs, cs] = (u[rs, cs] * mixed).astype(BF16)

    ys = (_dot(att_ref[...], woa_ref[...]), _dot(yb_ref[...], wob_ref[...]),
          _dot(yc_ref[...], woc_ref[...]), _dot(yd_ref[...], wod_ref[...]))
    merged = None
    for j in range(N_BRANCH):
        gl = pm_ref[:, COL_GATE + j * D_MODEL:COL_GATE + (j + 1) * D_MODEL].astype(F32)
        t = jax.nn.sigmoid(gl + gateb_ref[j:j + 1, :]) * ys[j]
        merged = t if merged is None else merged + t
    mo = _dot(merged.astype(BF16), wout_ref[...])
    out_ref[...] = x_ref[...] + _rms(mo, npost_ref[...])


def _mix(pm, att, x2, lw, batch, seq):
    m = x2.shape[0]
    nt = seq // TM_MIX
    hb = TM_MIX // HALO
    last = m // HALO - 1
    tile = lambda b, i: (b * nt + i, 0)
    prev = lambda b, i: (jnp.maximum((b * nt + i) * hb - 1, 0), 0)
    nxt = lambda b, i: (jnp.minimum((b * nt + i + 1) * hb, last), 0)
    return pl.pallas_call(
        _mix_kernel,
        grid=(batch, nt),
        in_specs=[
            pl.BlockSpec((TM_MIX, N_MAIN), tile),
            pl.BlockSpec((HALO, HALO_COLS), prev),
            pl.BlockSpec((HALO, HALO_COLS), nxt),
            pl.BlockSpec((TM_MIX, MLA_HEADS * V_HEAD), tile),
            pl.BlockSpec((TM_MIX, D_MODEL), tile),
            _resident((MLA_HEADS * V_HEAD, D_MODEL)),
            _resident((SC_WIDTH, D_MODEL)),
            _resident((CF_WIDTH, D_MODEL)),
            _resident((GM_WIDTH, D_MODEL)),
            _resident((D_MODEL, D_MODEL)),
            _resident((SC_K, SC_WIDTH)),
            _resident((CF_K, CF_WIDTH)),
            _resident((1, CF_WIDTH)),
            _resident((1, CF_WIDTH)),
            _resident((1, CF_WIDTH)),
            _resident((1, GM_WIDTH)),
            _resident((1, GM_WIDTH)),
            _resident((GM_GROUPS, GM_CHUNK, GM_CHUNK)),
            _resident((GM_CHUNK, GM_GROUPS)),
            _resident((N_BRANCH, D_MODEL)),
            _resident((1, D_MODEL)),
        ],
        out_specs=pl.BlockSpec((TM_MIX, D_MODEL), tile),
        out_shape=jax.ShapeDtypeStruct((m, D_MODEL), F32),
        scratch_shapes=[
            pltpu.VMEM((TM_MIX + 2 * HALO, CF_WIDTH), F32),
            pltpu.VMEM((TM_MIX + 2 * HALO, SC_WIDTH), F32),
            pltpu.VMEM((TM_MIX, SC_WIDTH), BF16),
            pltpu.VMEM((TM_MIX, CF_WIDTH), BF16),
            pltpu.VMEM((TM_MIX, GM_WIDTH), BF16),
        ],
        compiler_params=_params("parallel", "arbitrary"),
    )(pm, pm, pm, att, x2, lw["w_o_mla"], lw["w_o_sc"], lw["w_o_cf"], lw["w_o_gm"], lw["w_out"],
      lw["sc_conv_w"], lw["cf_conv_w"], lw["cf_conv_b"], lw["cf_ln_g"], lw["cf_ln_b"],
      lw["gm_ln_g"], lw["gm_ln_b"], lw["gm_ws"], lw["gm_bs_t"], lw["gate_b"], lw["norm_mix_post"])


def _ffn_kernel(x_ref, gpre_ref, wg_ref, wu_ref, wo_ref, gpost_ref, out_ref):
    x = x_ref[...]
    h = _rms(x, gpre_ref[...]).astype(BF16)
    acc = None
    for c in range(FFN_HIDDEN // FFN_CHUNK):
        cs = slice(c * FFN_CHUNK, (c + 1) * FFN_CHUNK)
        fg = _dot(h, wg_ref[:, cs])
        fu = _dot(h, wu_ref[:, cs])
        f = (fg * jax.nn.sigmoid(fg) * fu).astype(BF16)
        t = _dot(f, wo_ref[cs, :])
        acc = t if acc is None else acc + t
    out_ref[...] = x + _rms(acc, gpost_ref[...])


def _ffn(x2, lw):
    m = x2.shape[0]
    row = lambda i: (i, 0)
    return pl.pallas_call(
        _ffn_kernel,
        grid=(m // TM_FFN,),
        in_specs=[
            pl.BlockSpec((TM_FFN, D_MODEL), row),
            _resident((1, D_MODEL)),
            _resident((D_MODEL, FFN_HIDDEN)),
            _resident((D_MODEL, FFN_HIDDEN)),
            _resident((FFN_HIDDEN, D_MODEL)),
            _resident((1, D_MODEL)),
        ],
        out_specs=pl.BlockSpec((TM_FFN, D_MODEL), row),
        out_shape=jax.ShapeDtypeStruct((m, D_MODEL), F32),
        compiler_params=_params("parallel"),
    )(x2, lw["norm_ffn_pre"], lw["w_ffn_g"], lw["w_ffn_u"], lw["w_ffn_out"], lw["norm_ffn_post"])


def _head_slab(w, src_cols, dst_off, width, sign=1.0):
    rows = w.shape[0]
    out = jnp.zeros((rows, MLA_HEADS, HEAD_PAD), F32)
    src = w.reshape(rows, MLA_HEADS, -1)[:, :, src_cols:src_cols + width]
    return out.at[:, :, dst_off:dst_off + width].set(sign * src)


def _rope_place():
    half = QK_ROPE // 2
    plain = np.zeros((LANES, MLA_HEADS, HEAD_PAD), np.float32)
    rot = np.zeros((LANES, MLA_HEADS, HEAD_PAD), np.float32)
    for j in range(QK_ROPE):
        plain[j, :, QK_NOPE + j] = 1.0
    for j in range(half):
        rot[half + j, :, QK_NOPE + j] = -1.0
        rot[j, :, QK_NOPE + half + j] = 1.0
    shape = (LANES, MLA_HEADS * HEAD_PAD)
    return jnp.asarray(plain.reshape(shape), BF16), jnp.asarray(rot.reshape(shape), BF16)


def _layer_weights(l, p):
    hp = MLA_HEADS * HEAD_PAD
    half = QK_ROPE // 2
    w_in = p["w_in"][l]
    o = Q_LORA + KV_LORA + QK_ROPE

    def cols(k):
        return w_in[:, o + k * 512:o + (k + 1) * 512]

    sc_b, sc_c, sc_x, cf_a, cf_g, gm_u, gm_v = (cols(k) for k in range(7))
    gates = w_in[:, o + 7 * 512:]
    w_main = jnp.concatenate([cf_a, cf_g, sc_c, sc_x, sc_b, gm_u, gm_v, gates], axis=1).astype(BF16)
    w_c = jnp.concatenate([w_in[:, :o], jnp.zeros((D_MODEL, N_LATENT - o), F32)], axis=1).astype(BF16)

    w_uq = p["w_uq"][l]
    wq_a = _head_slab(w_uq, 0, 0, QK_NOPE + QK_ROPE)
    wq_b = (_head_slab(w_uq, QK_NOPE + half, QK_NOPE, half, -1.0)
            + _head_slab(w_uq, QK_NOPE, QK_NOPE + half, half))
    w_ukv = p["w_ukv"][l]
    wk_a = _head_slab(w_ukv, 0, 0, QK_NOPE)
    wv = _head_slab(w_ukv, QK_NOPE, 0, V_HEAD)
    wk_r, wk_b = _rope_place()
    v_one = np.zeros((1, MLA_HEADS, HEAD_PAD), np.float32)
    v_one[:, :, V_HEAD] = 1.0

    w_ffn_in = p["w_ffn_in"][l]
    row = lambda a: a[l].reshape(1, -1)
    return {
        "norm_mix_pre": row(p["norm_mix_pre"]),
        "w_main": w_main,
        "w_c": w_c,
        "q_norm": row(p["mla_q_norm"]),
        "kv_norm": row(p["mla_kv_norm"]),
        "wq_a": wq_a.reshape(Q_LORA, hp).astype(BF16),
        "wq_b": wq_b.reshape(Q_LORA, hp).astype(BF16),
        "wk_a": wk_a.reshape(KV_LORA, hp).astype(BF16),
        "wk_r": wk_r,
        "wk_b": wk_b,
        "wv": wv.reshape(KV_LORA, hp).astype(BF16),
        "v_one": jnp.asarray(v_one.reshape(1, hp)),
        "w_o_mla": p["w_o_mla"][l].astype(BF16),
        "w_o_sc": p["w_o_sc"][l].astype(BF16),
        "w_o_cf": p["w_o_cf"][l].astype(BF16),
        "w_o_gm": p["w_o_gm"][l].astype(BF16),
        "w_out": p["w_out"][l].astype(BF16),
        "sc_conv_w": p["sc_conv_w"][l],
        "cf_conv_w": p["cf_conv_w"][l],
        "cf_conv_b": row(p["cf_conv_b"]),
        "cf_ln_g": row(p["cf_ln_g"]),
        "cf_ln_b": row(p["cf_ln_b"]),
        "gm_ln_g": row(p["gm_ln_g"]),
        "gm_ln_b": row(p["gm_ln_b"]),
        "gm_ws": p["gm_ws"][l].astype(BF16),
        "gm_bs_t": p["gm_bs"][l].T,
        "gate_b": p["gate_b"][l],
        "norm_mix_post": row(p["norm_mix_post"]),
        "norm_ffn_pre": row(p["norm_ffn_pre"]),
        "w_ffn_g": w_ffn_in[:, :FFN_HIDDEN].astype(BF16),
        "w_ffn_u": w_ffn_in[:, FFN_HIDDEN:].astype(BF16),
        "w_ffn_out": p["w_ffn_out"][l].astype(BF16),
        "norm_ffn_post": row(p["norm_ffn_post"]),
    }


def _rope_tables(positions):
    m = positions.size
    inv_freq = ROPE_THETA ** (-jnp.arange(0, QK_ROPE, 2, dtype=F32) / QK_ROPE)
    ang = positions.astype(F32).reshape(m, 1) * inv_freq
    cos, sin = jnp.cos(ang), jnp.sin(ang)
    pad = jnp.zeros((m, HEAD_PAD - QK_NOPE - QK_ROPE), F32)
    cos_t = jnp.concatenate([jnp.ones((m, QK_NOPE), F32), cos, cos, pad], axis=1)
    sin_t = jnp.concatenate([jnp.zeros((m, QK_NOPE), F32), sin, sin, pad], axis=1)
    return cos_t, sin_t


def kernel(x, positions, norm_mix_pre, w_in, mla_q_norm, w_uq, mla_kv_norm, w_ukv, w_o_mla, sc_conv_w, w_o_sc, cf_conv_w, cf_conv_b, cf_ln_g, cf_ln_b, w_o_cf, gm_ln_g, gm_ln_b, gm_ws, gm_bs, w_o_gm, gate_b, w_out, norm_mix_post, norm_ffn_pre, w_ffn_in, w_ffn_out, norm_ffn_post):
    p = dict(norm_mix_pre=norm_mix_pre, w_in=w_in, mla_q_norm=mla_q_norm, w_uq=w_uq,
             mla_kv_norm=mla_kv_norm, w_ukv=w_ukv, w_o_mla=w_o_mla, sc_conv_w=sc_conv_w,
             w_o_sc=w_o_sc, cf_conv_w=cf_conv_w, cf_conv_b=cf_conv_b, cf_ln_g=cf_ln_g,
             cf_ln_b=cf_ln_b, w_o_cf=w_o_cf, gm_ln_g=gm_ln_g, gm_ln_b=gm_ln_b, gm_ws=gm_ws,
             gm_bs=gm_bs, w_o_gm=w_o_gm, gate_b=gate_b, w_out=w_out, norm_mix_post=norm_mix_post,
             norm_ffn_pre=norm_ffn_pre, w_ffn_in=w_ffn_in, w_ffn_out=w_ffn_out,
             norm_ffn_post=norm_ffn_post)
    batch, seq, d = x.shape
    depth = w_in.shape[0]
    m = batch * seq
    assert d == D_MODEL and m % TM_INPROJ == 0 and seq % TQ_ATTN == 0 and seq % TM_MIX == 0
    cos_t, sin_t = _rope_tables(positions)
    x2 = x.reshape(m, d)
    for l in range(depth):
        lw = _layer_weights(l, p)
        pm = _inproj(x2, lw["norm_mix_pre"], lw["w_main"])
        q, k, v = _prep(x2, lw["norm_mix_pre"], lw, cos_t, sin_t)
        att = _attention(q, k, v, batch, seq)
        x2 = _mix(pm, att, x2, lw, batch, seq)
        x2 = _ffn(x2, lw)
    return x2.reshape(batch, seq, d)
```

```python
import functools

import jax
import jax.numpy as jnp
import numpy as np
from jax import lax
from jax.experimental import pallas as pl
from jax.experimental.pallas import tpu as pltpu

F32 = jnp.float32
BF16 = jnp.bfloat16

D_MODEL = 1024
MLA_HEADS = 8
QK_NOPE = 64
QK_ROPE = 32
V_HEAD = 64
Q_LORA = 256
KV_LORA = 256
ROPE_THETA = 10000.0
SC_WIDTH = 512
SC_K = 3
CF_WIDTH = 512
CF_K = 31
GM_GROUPS = 4
GM_WIDTH = 512
GM_CHUNK = 128
N_BRANCH = 4
FFN_HIDDEN = 2816
EPS = 1e-6

LANES = 128
HEAD_PAD = LANES
HALO = 16
SUBLANES = 8
CONV_ROWS = 64

COL_CF_A = 0
COL_CF_G = 512
COL_SC_C = 1024
COL_SC_X = 1536
COL_SC_B = 2048
COL_GM_U = 2560
COL_GM_V = 3072
COL_GATE = 3584
N_MAIN = COL_GATE + N_BRANCH * D_MODEL
HALO_COLS = COL_SC_B
N_LATENT = 640

TM_INPROJ = 1024
TN_INPROJ = 1536
TM_PREP = 512
QB_ATTN = 256
KC_ATTN = 256
TM_MIX = 256
TM_FFN = 512
FFN_CHUNK = 1408
VMEM_LIMIT = 56 * 1024 * 1024


def _rms(x, g):
    ms = jnp.mean(x * x, axis=-1, keepdims=True)
    return x * lax.rsqrt(ms + EPS) * g


def _layer_norm(x, g, b):
    mu = jnp.mean(x, axis=-1, keepdims=True)
    xc = x - mu
    var = jnp.mean(xc * xc, axis=-1, keepdims=True)
    return xc * lax.rsqrt(var + EPS) * g + b


def _dot(a, b):
    return jnp.dot(a, b, preferred_element_type=F32)


_NT = (((1,), (1,)), ((), ()))
LOG2_E = 1.4426950408889634


def _resident(shape):
    nd = len(shape)
    return pl.BlockSpec(shape, lambda *_: (0,) * nd, pipeline_mode=pl.Buffered(1))


def _params(*sem):
    return pltpu.CompilerParams(dimension_semantics=sem, vmem_limit_bytes=VMEM_LIMIT)


def _inproj_kernel(x_ref, g_ref, w_ref, o_ref, h_ref):
    @pl.when(pl.program_id(1) == 0)
    def _():
        h_ref[...] = _rms(x_ref[...], g_ref[...]).astype(BF16)

    o_ref[...] = _dot(h_ref[...], w_ref[...]).astype(o_ref.dtype)


def _inproj(x2, g, w_main):
    m = x2.shape[0]
    return pl.pallas_call(
        _inproj_kernel,
        grid=(m // TM_INPROJ, N_MAIN // TN_INPROJ),
        in_specs=[
            pl.BlockSpec((TM_INPROJ, D_MODEL), lambda i, j: (i, 0)),
            pl.BlockSpec((1, D_MODEL), lambda i, j: (0, 0)),
            pl.BlockSpec((D_MODEL, TN_INPROJ), lambda i, j: (0, j)),
        ],
        out_specs=pl.BlockSpec((TM_INPROJ, TN_INPROJ), lambda i, j: (i, j)),
        out_shape=jax.ShapeDtypeStruct((m, N_MAIN), BF16),
        scratch_shapes=[pltpu.VMEM((TM_INPROJ, D_MODEL), BF16)],
        compiler_params=_params("parallel", "arbitrary"),
    )(x2, g, w_main)


def _prep_kernel(x_ref, g_ref, wc_ref, qn_ref, kvn_ref, wqa_ref, wqb_ref, wka_ref, wkr_ref,
                 wkb_ref, wvt_ref, vone_ref, cos_ref, sin_ref, q_ref, k_ref, vt_ref):
    h = _rms(x_ref[...], g_ref[...]).astype(BF16)
    c = _dot(h, wc_ref[...])
    cq = _rms(c[:, :Q_LORA], qn_ref[...]).astype(BF16)
    ckv = _rms(c[:, Q_LORA:Q_LORA + KV_LORA], kvn_ref[...]).astype(BF16)
    kr = c[:, Q_LORA + KV_LORA:].astype(BF16)
    cos = cos_ref[...]
    sin = sin_ref[...]
    scale = (QK_NOPE + QK_ROPE) ** -0.5 * LOG2_E
    qa = _dot(cq, wqa_ref[...])
    qb = _dot(cq, wqb_ref[...])
    ka = _dot(ckv, wka_ref[...]) + _dot(kr, wkr_ref[...])
    kb = _dot(kr, wkb_ref[...])
    for hd in range(MLA_HEADS):
        sl = slice(hd * HEAD_PAD, (hd + 1) * HEAD_PAD)
        q_ref[:, sl] = ((qa[:, sl] * cos + qb[:, sl] * sin) * scale).astype(BF16)
        k_ref[:, sl] = (ka[:, sl] * cos + kb[:, sl] * sin).astype(BF16)
    vt = lax.dot_general(wvt_ref[...], ckv, _NT, preferred_element_type=F32)
    vt_ref[...] = (vt + vone_ref[...]).astype(BF16)


def _prep(x2, g, lw, cos_t, sin_t):
    m = x2.shape[0]
    hp = MLA_HEADS * HEAD_PAD
    row = lambda i: (i, 0)
    slab = jax.ShapeDtypeStruct((m, hp), BF16)
    return pl.pallas_call(
        _prep_kernel,
        grid=(m // TM_PREP,),
        in_specs=[
            pl.BlockSpec((TM_PREP, D_MODEL), row),
            _resident((1, D_MODEL)),
            _resident((D_MODEL, N_LATENT)),
            _resident((1, Q_LORA)),
            _resident((1, KV_LORA)),
            _resident((Q_LORA, hp)),
            _resident((Q_LORA, hp)),
            _resident((KV_LORA, hp)),
            _resident((LANES, hp)),
            _resident((LANES, hp)),
            _resident((hp, KV_LORA)),
            _resident((hp, 1)),
            pl.BlockSpec((TM_PREP, LANES), row),
            pl.BlockSpec((TM_PREP, LANES), row),
        ],
        out_specs=[pl.BlockSpec((TM_PREP, hp), row), pl.BlockSpec((TM_PREP, hp), row),
                   pl.BlockSpec((hp, TM_PREP), lambda i: (0, i))],
        out_shape=[slab, slab, jax.ShapeDtypeStruct((hp, m), BF16)],
        compiler_params=_params("parallel"),
    )(x2, g, lw["w_c"], lw["q_norm"], lw["kv_norm"], lw["wq_a"], lw["wq_b"], lw["wk_a"],
      lw["wk_r"], lw["wk_b"], lw["wv_t"], lw["v_one"], cos_t, sin_t)


def _attn_kernel(q_ref, k_ref, vt_ref, o_ref, *bufs):
    seq = k_ref.shape[0]
    n_kc = seq // KC_ATTN
    n_qb = seq // QB_ATTN
    heads = [slice(hh * HEAD_PAD, (hh + 1) * HEAD_PAD) for hh in range(2)]

    def scores_chunk(blk, s_ref, c, run_max):
        keys = slice(c * KC_ATTN, (c + 1) * KC_ATTN)
        q_rows = slice(blk * QB_ATTN, (blk + 1) * QB_ATTN)
        for hh, sl in enumerate(heads):
            st = lax.dot_general(k_ref[keys, sl], q_ref[q_rows, sl], _NT,
                                 preferred_element_type=F32)
            s_ref[hh, keys, :] = st
            cm = run_max[hh]
            for r in range(KC_ATTN // SUBLANES):
                part = st[r * SUBLANES:(r + 1) * SUBLANES, :]
                cm = part if cm is None else jnp.maximum(cm, part)
            run_max[hh] = cm

    def values_chunk(s_ref, c, row_max, acc):
        keys = slice(c * KC_ATTN, (c + 1) * KC_ATTN)
        for hh, sl in enumerate(heads):
            pt = jnp.exp2(s_ref[hh, keys, :] - row_max[hh]).astype(BF16)
            t = _dot(vt_ref[sl, keys], pt)
            acc[hh] = t if acc[hh] is None else acc[hh] + t

    ahead = len(bufs) - 1
    maxes = {}
    for blk in range(min(ahead, n_qb)):
        maxes[blk] = [None, None]
        for c in range(n_kc):
            scores_chunk(blk, bufs[blk % len(bufs)], c, maxes[blk])
    for blk in range(n_qb):
        row_max = [jnp.max(mx, axis=0, keepdims=True) for mx in maxes.pop(blk)]
        acc = [None, None]
        nxt = blk + ahead
        if nxt < n_qb:
            maxes[nxt] = [None, None]
        for c in range(n_kc):
            if nxt < n_qb:
                scores_chunk(nxt, bufs[nxt % len(bufs)], c, maxes[nxt])
            values_chunk(bufs[blk % len(bufs)], c, row_max, acc)
        halves = [(a * (1.0 / a[V_HEAD:V_HEAD + 1, :]))[:V_HEAD, :] for a in acc]
        o_ref[blk * QB_ATTN:(blk + 1) * QB_ATTN, :] = jnp.concatenate(halves, axis=0).T.astype(BF16)


def _attention(q, k, vt, batch, seq):
    m = q.shape[0]
    pair = 2 * HEAD_PAD
    return pl.pallas_call(
        _attn_kernel,
        grid=(batch, MLA_HEADS // 2),
        in_specs=[
            pl.BlockSpec((seq, pair), lambda b, p: (b, p)),
            pl.BlockSpec((seq, pair), lambda b, p: (b, p)),
            pl.BlockSpec((pair, seq), lambda b, p: (p, b)),
        ],
        out_specs=pl.BlockSpec((seq, 2 * V_HEAD), lambda b, p: (b, p)),
        out_shape=jax.ShapeDtypeStruct((m, MLA_HEADS * V_HEAD), BF16),
        scratch_shapes=[
            pltpu.VMEM((2, seq, QB_ATTN), F32),
            pltpu.VMEM((2, seq, QB_ATTN), F32),
            pltpu.VMEM((2, seq, QB_ATTN), F32),
        ],
        compiler_params=_params("parallel", "parallel"),
    )(q, k, vt)


def _depthwise_rows(z_ref, w_ref, bias, out_ref, r0, ls, ntaps):
    nvo = CONV_ROWS // SUBLANES
    shifts = [HALO - ntaps // 2 + k for k in range(ntaps)]
    lo = min(shifts) // SUBLANES
    hi = (max(shifts) + SUBLANES - 1) // SUBLANES + nvo - 1
    win = {i: z_ref[pl.ds(r0 + i * SUBLANES, SUBLANES), ls] for i in range(lo, hi + 1)}
    sub = lax.broadcasted_iota(jnp.int32, (SUBLANES, LANES), 0)
    acc = [bias] * nvo
    for r in range(SUBLANES):
        taps = [k for k in range(ntaps) if shifts[k] % SUBLANES == r]
        if not taps:
            continue
        need = sorted({shifts[k] // SUBLANES + j for k in taps for j in range(nvo)})
        if r == 0:
            sh = {i: win[i] for i in need}
        else:
            rolled = {i: pltpu.roll(win[i], SUBLANES - r, 0)
                      for i in sorted(set(need) | {i + 1 for i in need})}
            sh = {i: jnp.where(sub < SUBLANES - r, rolled[i], rolled[i + 1]) for i in need}
        for k in taps:
            a = shifts[k] // SUBLANES
            wv = w_ref[k, :, ls]
            for j in range(nvo):
                t = sh[a + j] * wv
                acc[j] = t if acc[j] is None else acc[j] + t
    for j in range(nvo):
        out_ref[pl.ds(r0 + j * SUBLANES, SUBLANES), ls] = acc[j]


def _mix_kernel(pm_ref, hp_ref, hn_ref, att_ref, x_ref, woa_ref, wob_ref, woc_ref, wod_ref,
                wout_ref, scw_ref, cfw_ref, cfb_ref, cflg_ref, cflb_ref, gmlg_ref, gmlb_ref,
                gmws_ref, gmbs_ref, gateb_ref, npost_ref, out_ref,
                zc_ref, zs_ref, yb_ref, yc_ref, yd_ref):
    tm = TM_MIX
    i = pl.program_id(1)
    has_prev = (i > 0).astype(F32)
    has_next = (i < pl.num_programs(1) - 1).astype(F32)

    def glu(ref, rows):
        a = ref[rows, COL_CF_A:COL_CF_A + CF_WIDTH].astype(F32)
        g = ref[rows, COL_CF_G:COL_CF_G + CF_WIDTH].astype(F32)
        return a * jax.nn.sigmoid(g)

    def scz(ref, rows):
        c = ref[rows, COL_SC_C:COL_SC_C + SC_WIDTH].astype(F32)
        x = ref[rows, COL_SC_X:COL_SC_X + SC_WIDTH].astype(F32)
        return c * x

    every = slice(None)
    zc_ref[0:HALO, :] = glu(hp_ref, every) * has_prev
    zc_ref[HALO:HALO + tm, :] = glu(pm_ref, every)
    zc_ref[HALO + tm:, :] = glu(hn_ref, every) * has_next
    zs_ref[0:HALO, :] = scz(hp_ref, every) * has_prev
    zs_ref[HALO:HALO + tm, :] = scz(pm_ref, every)
    zs_ref[HALO + tm:, :] = scz(hn_ref, every) * has_next

    def conv_step(c, carry):
        r0 = pl.multiple_of(c * CONV_ROWS, CONV_ROWS)
        for lt in range(CF_WIDTH // LANES):
            ls = slice(lt * LANES, (lt + 1) * LANES)
            _depthwise_rows(zc_ref, cfw_ref, cfb_ref[:, ls], yc_ref, r0, ls, CF_K)
            _depthwise_rows(zs_ref, scw_ref, None, yb_ref, r0, ls, SC_K)
        return carry

    lax.fori_loop(0, tm // CONV_ROWS, conv_step, 0)
    yc = _layer_norm(yc_ref[...], cflg_ref[...], cflb_ref[...])
    yc = (yc * jax.nn.sigmoid(yc)).astype(BF16)
    yb = (pm_ref[:, COL_SC_B:COL_SC_B + SC_WIDTH].astype(F32) * yb_ref[...]).astype(BF16)

    u = jax.nn.gelu(pm_ref[:, COL_GM_U:COL_GM_U + GM_WIDTH].astype(F32))
    v = jax.nn.gelu(pm_ref[:, COL_GM_V:COL_GM_V + GM_WIDTH].astype(F32))
    v = _layer_norm(v, gmlg_ref[...], gmlb_ref[...]).astype(BF16)
    gw = GM_WIDTH // GM_GROUPS
    for n in range(tm // GM_CHUNK):
        rs = slice(n * GM_CHUNK, (n + 1) * GM_CHUNK)
        for g in range(GM_GROUPS):
            cs = slice(g * gw, (g + 1) * gw)
            mixed = _dot(gmws_ref[g], v[rs, cs]) + gmbs_ref[:, g:g + 1]
            yd_ref[rs, cs] = (u[rs, cs] * mixed).astype(BF16)

    ys = (_dot(att_ref[...], woa_ref[...]), _dot(yb, wob_ref[...]),
          _dot(yc, woc_ref[...]), _dot(yd_ref[...], wod_ref[...]))
    merged = None
    for j in range(N_BRANCH):
        gl = pm_ref[:, COL_GATE + j * D_MODEL:COL_GATE + (j + 1) * D_MODEL].astype(F32)
        t = jax.nn.sigmoid(gl + gateb_ref[j:j + 1, :]) * ys[j]
        merged = t if merged is None else merged + t
    mo = _dot(merged.astype(BF16), wout_ref[...])
    out_ref[...] = x_ref[...] + _rms(mo, npost_ref[...])


def _mix(pm, att, x2, lw, batch, seq):
    m = x2.shape[0]
    nt = seq // TM_MIX
    hb = TM_MIX // HALO
    last = m // HALO - 1
    tile = lambda b, i: (b * nt + i, 0)
    prev = lambda b, i: (jnp.maximum((b * nt + i) * hb - 1, 0), 0)
    nxt = lambda b, i: (jnp.minimum((b * nt + i + 1) * hb, last), 0)
    return pl.pallas_call(
        _mix_kernel,
        grid=(batch, nt),
        in_specs=[
            pl.BlockSpec((TM_MIX, N_MAIN), tile),
            pl.BlockSpec((HALO, HALO_COLS), prev),
            pl.BlockSpec((HALO, HALO_COLS), nxt),
            pl.BlockSpec((TM_MIX, MLA_HEADS * V_HEAD), tile),
            pl.BlockSpec((TM_MIX, D_MODEL), tile),
            _resident((MLA_HEADS * V_HEAD, D_MODEL)),
            _resident((SC_WIDTH, D_MODEL)),
            _resident((CF_WIDTH, D_MODEL)),
            _resident((GM_WIDTH, D_MODEL)),
            _resident((D_MODEL, D_MODEL)),
            _resident((SC_K, SUBLANES, SC_WIDTH)),
            _resident((CF_K, SUBLANES, CF_WIDTH)),
            _resident((SUBLANES, CF_WIDTH)),
            _resident((1, CF_WIDTH)),
            _resident((1, CF_WIDTH)),
            _resident((1, GM_WIDTH)),
            _resident((1, GM_WIDTH)),
            _resident((GM_GROUPS, GM_CHUNK, GM_CHUNK)),
            _resident((GM_CHUNK, GM_GROUPS)),
            _resident((N_BRANCH, D_MODEL)),
            _resident((1, D_MODEL)),
        ],
        out_specs=pl.BlockSpec((TM_MIX, D_MODEL), tile),
        out_shape=jax.ShapeDtypeStruct((m, D_MODEL), F32),
        scratch_shapes=[
            pltpu.VMEM((TM_MIX + 2 * HALO, CF_WIDTH), F32),
            pltpu.VMEM((TM_MIX + 2 * HALO, SC_WIDTH), F32),
            pltpu.VMEM((TM_MIX, SC_WIDTH), F32),
            pltpu.VMEM((TM_MIX, CF_WIDTH), F32),
            pltpu.VMEM((TM_MIX, GM_WIDTH), BF16),
        ],
        compiler_params=_params("parallel", "arbitrary"),
    )(pm, pm, pm, att, x2, lw["w_o_mla"], lw["w_o_sc"], lw["w_o_cf"], lw["w_o_gm"], lw["w_out"],
      lw["sc_conv_w"], lw["cf_conv_w"], lw["cf_conv_b"], lw["cf_ln_g"], lw["cf_ln_b"],
      lw["gm_ln_g"], lw["gm_ln_b"], lw["gm_ws"], lw["gm_bs_t"], lw["gate_b"], lw["norm_mix_post"])


def _ffn_kernel(x_ref, gpre_ref, wg_ref, wu_ref, wo_ref, gpost_ref, out_ref):
    x = x_ref[...]
    h = _rms(x, gpre_ref[...]).astype(BF16)
    acc = None
    for c in range(FFN_HIDDEN // FFN_CHUNK):
        cs = slice(c * FFN_CHUNK, (c + 1) * FFN_CHUNK)
        fg = _dot(h, wg_ref[:, cs])
        fu = _dot(h, wu_ref[:, cs])
        f = (fg * jax.nn.sigmoid(fg) * fu).astype(BF16)
        t = _dot(f, wo_ref[cs, :])
        acc = t if acc is None else acc + t
    out_ref[...] = x + _rms(acc, gpost_ref[...])


def _ffn(x2, lw):
    m = x2.shape[0]
    row = lambda i: (i, 0)
    return pl.pallas_call(
        _ffn_kernel,
        grid=(m // TM_FFN,),
        in_specs=[
            pl.BlockSpec((TM_FFN, D_MODEL), row),
            _resident((1, D_MODEL)),
            _resident((D_MODEL, FFN_HIDDEN)),
            _resident((D_MODEL, FFN_HIDDEN)),
            _resident((FFN_HIDDEN, D_MODEL)),
            _resident((1, D_MODEL)),
        ],
        out_specs=pl.BlockSpec((TM_FFN, D_MODEL), row),
        out_shape=jax.ShapeDtypeStruct((m, D_MODEL), F32),
        compiler_params=_params("parallel"),
    )(x2, lw["norm_ffn_pre"], lw["w_ffn_g"], lw["w_ffn_u"], lw["w_ffn_out"], lw["norm_ffn_post"])


def _head_slab(w, src_cols, dst_off, width, sign=1.0):
    rows = w.shape[0]
    out = jnp.zeros((rows, MLA_HEADS, HEAD_PAD), F32)
    src = w.reshape(rows, MLA_HEADS, -1)[:, :, src_cols:src_cols + width]
    return out.at[:, :, dst_off:dst_off + width].set(sign * src)


def _rope_place():
    half = QK_ROPE // 2
    plain = np.zeros((LANES, MLA_HEADS, HEAD_PAD), np.float32)
    rot = np.zeros((LANES, MLA_HEADS, HEAD_PAD), np.float32)
    for j in range(QK_ROPE):
        plain[j, :, QK_NOPE + j] = 1.0
    for j in range(half):
        rot[half + j, :, QK_NOPE + j] = -1.0
        rot[j, :, QK_NOPE + half + j] = 1.0
    shape = (LANES, MLA_HEADS * HEAD_PAD)
    return jnp.asarray(plain.reshape(shape), BF16), jnp.asarray(rot.reshape(shape), BF16)


def _layer_weights(l, p):
    hp = MLA_HEADS * HEAD_PAD
    half = QK_ROPE // 2
    w_in = p["w_in"][l]
    o = Q_LORA + KV_LORA + QK_ROPE

    def cols(k):
        return w_in[:, o + k * 512:o + (k + 1) * 512]

    sc_b, sc_c, sc_x, cf_a, cf_g, gm_u, gm_v = (cols(k) for k in range(7))
    gates = w_in[:, o + 7 * 512:]
    w_main = jnp.concatenate([cf_a, cf_g, sc_c, sc_x, sc_b, gm_u, gm_v, gates], axis=1).astype(BF16)
    w_c = jnp.concatenate([w_in[:, :o], jnp.zeros((D_MODEL, N_LATENT - o), F32)], axis=1).astype(BF16)

    w_uq = p["w_uq"][l]
    wq_a = _head_slab(w_uq, 0, 0, QK_NOPE + QK_ROPE)
    wq_b = (_head_slab(w_uq, QK_NOPE + half, QK_NOPE, half, -1.0)
            + _head_slab(w_uq, QK_NOPE, QK_NOPE + half, half))
    w_ukv = p["w_ukv"][l]
    wk_a = _head_slab(w_ukv, 0, 0, QK_NOPE)
    wv = _head_slab(w_ukv, QK_NOPE, 0, V_HEAD)
    wk_r, wk_b = _rope_place()
    v_one = np.zeros((1, MLA_HEADS, HEAD_PAD), np.float32)
    v_one[:, :, V_HEAD] = 1.0

    w_ffn_in = p["w_ffn_in"][l]
    row = lambda a: a[l].reshape(1, -1)

    def rows8(a):
        return jnp.broadcast_to(a[..., None, :], a.shape[:-1] + (SUBLANES, a.shape[-1]))

    return {
        "norm_mix_pre": row(p["norm_mix_pre"]),
        "w_main": w_main,
        "w_c": w_c,
        "q_norm": row(p["mla_q_norm"]),
        "kv_norm": row(p["mla_kv_norm"]),
        "wq_a": wq_a.reshape(Q_LORA, hp).astype(BF16),
        "wq_b": wq_b.reshape(Q_LORA, hp).astype(BF16),
        "wk_a": wk_a.reshape(KV_LORA, hp).astype(BF16),
        "wk_r": wk_r,
        "wk_b": wk_b,
        "wv_t": wv.reshape(KV_LORA, hp).T.astype(BF16),
        "v_one": jnp.asarray(v_one.reshape(hp, 1)),
        "w_o_mla": p["w_o_mla"][l].astype(BF16),
        "w_o_sc": p["w_o_sc"][l].astype(BF16),
        "w_o_cf": p["w_o_cf"][l].astype(BF16),
        "w_o_gm": p["w_o_gm"][l].astype(BF16),
        "w_out": p["w_out"][l].astype(BF16),
        "sc_conv_w": rows8(p["sc_conv_w"][l]),
        "cf_conv_w": rows8(p["cf_conv_w"][l]),
        "cf_conv_b": rows8(p["cf_conv_b"][l]),
        "cf_ln_g": row(p["cf_ln_g"]),
        "cf_ln_b": row(p["cf_ln_b"]),
        "gm_ln_g": row(p["gm_ln_g"]),
        "gm_ln_b": row(p["gm_ln_b"]),
        "gm_ws": p["gm_ws"][l].astype(BF16),
        "gm_bs_t": p["gm_bs"][l].T,
        "gate_b": p["gate_b"][l],
        "norm_mix_post": row(p["norm_mix_post"]),
        "norm_ffn_pre": row(p["norm_ffn_pre"]),
        "w_ffn_g": w_ffn_in[:, :FFN_HIDDEN].astype(BF16),
        "w_ffn_u": w_ffn_in[:, FFN_HIDDEN:].astype(BF16),
        "w_ffn_out": p["w_ffn_out"][l].astype(BF16),
        "norm_ffn_post": row(p["norm_ffn_post"]),
    }


def _rope_tables(positions):
    m = positions.size
    inv_freq = ROPE_THETA ** (-jnp.arange(0, QK_ROPE, 2, dtype=F32) / QK_ROPE)
    ang = positions.astype(F32).reshape(m, 1) * inv_freq
    cos, sin = jnp.cos(ang), jnp.sin(ang)
    pad = jnp.zeros((m, HEAD_PAD - QK_NOPE - QK_ROPE), F32)
    cos_t = jnp.concatenate([jnp.ones((m, QK_NOPE), F32), cos, cos, pad], axis=1)
    sin_t = jnp.concatenate([jnp.zeros((m, QK_NOPE), F32), sin, sin, pad], axis=1)
    return cos_t, sin_t


def kernel(x, positions, norm_mix_pre, w_in, mla_q_norm, w_uq, mla_kv_norm, w_ukv, w_o_mla, sc_conv_w, w_o_sc, cf_conv_w, cf_conv_b, cf_ln_g, cf_ln_b, w_o_cf, gm_ln_g, gm_ln_b, gm_ws, gm_bs, w_o_gm, gate_b, w_out, norm_mix_post, norm_ffn_pre, w_ffn_in, w_ffn_out, norm_ffn_post):
    p = dict(norm_mix_pre=norm_mix_pre, w_in=w_in, mla_q_norm=mla_q_norm, w_uq=w_uq,
             mla_kv_norm=mla_kv_norm, w_ukv=w_ukv, w_o_mla=w_o_mla, sc_conv_w=sc_conv_w,
             w_o_sc=w_o_sc, cf_conv_w=cf_conv_w, cf_conv_b=cf_conv_b, cf_ln_g=cf_ln_g,
             cf_ln_b=cf_ln_b, w_o_cf=w_o_cf, gm_ln_g=gm_ln_g, gm_ln_b=gm_ln_b, gm_ws=gm_ws,
             gm_bs=gm_bs, w_o_gm=w_o_gm, gate_b=gate_b, w_out=w_out, norm_mix_post=norm_mix_post,
             norm_ffn_pre=norm_ffn_pre, w_ffn_in=w_ffn_in, w_ffn_out=w_ffn_out,
             norm_ffn_post=norm_ffn_post)
    batch, seq, d = x.shape
    depth = w_in.shape[0]
    m = batch * seq
    assert d == D_MODEL and m % TM_INPROJ == 0 and seq % QB_ATTN == 0 and seq % KC_ATTN == 0 and seq % TM_MIX == 0
    cos_t, sin_t = _rope_tables(positions)
    x2 = x.reshape(m, d)
    for l in range(depth):
        lw = _layer_weights(l, p)
        pm = _inproj(x2, lw["norm_mix_pre"], lw["w_main"])
        q, k, vt = _prep(x2, lw["norm_mix_pre"], lw, cos_t, sin_t)
        att = _attention(q, k, vt, batch, seq)
        x2 = _mix(pm, att, x2, lw, batch, seq)
        x2 = _ffn(x2, lw)
    return x2.reshape(batch, seq, d)
```

```python
import jax
import jax.numpy as jnp
import numpy as np
from jax import lax
from jax.experimental import pallas as pl
from jax.experimental.pallas import tpu as pltpu

F32 = jnp.float32
BF16 = jnp.bfloat16

D_MODEL = 1024
MLA_HEADS = 8
QK_NOPE = 64
QK_ROPE = 32
V_HEAD = 64
Q_LORA = 256
KV_LORA = 256
ROPE_THETA = 10000.0
SC_WIDTH = 512
SC_K = 3
CF_WIDTH = 512
CF_K = 31
GM_GROUPS = 4
GM_WIDTH = 512
GM_CHUNK = 128
N_BRANCH = 4
FFN_HIDDEN = 2816
EPS = 1e-6

LANES = 128
HEAD_PAD = LANES
HALO = 16
SUBLANES = 8
CONV_ROWS = 64

COL_CF_A = 0
COL_CF_G = 512
COL_SC_C = 1024
COL_SC_X = 1536
COL_SC_B = 2048
COL_GM_U = 2560
COL_GM_V = 3072
COL_GATE = 3584
N_MAIN = COL_GATE + N_BRANCH * D_MODEL
N_LATENT = 640

TM_PREP = 512
QB_ATTN = 256
KC_ATTN = 256
TM_MIX = 512
MERGE_COLS = 256
TM_FFN = 512
FFN_CHUNK = 1408
VMEM_LIMIT = 56 * 1024 * 1024


def _rms(x, g):
    ms = jnp.mean(x * x, axis=-1, keepdims=True)
    return x * lax.rsqrt(ms + EPS) * g


def _layer_norm(x, g, b):
    mu = jnp.mean(x, axis=-1, keepdims=True)
    xc = x - mu
    var = jnp.mean(xc * xc, axis=-1, keepdims=True)
    return xc * lax.rsqrt(var + EPS) * g + b


def _sigmoid(x):
    return 0.5 * jnp.tanh(0.5 * x) + 0.5


def _dot(a, b):
    return jnp.dot(a, b, preferred_element_type=F32)


_NT = (((1,), (1,)), ((), ()))
LOG2_E = 1.4426950408889634


def _resident(shape):
    nd = len(shape)
    return pl.BlockSpec(shape, lambda *_: (0,) * nd, pipeline_mode=pl.Buffered(1))


def _layer(shape, l):
    nd = len(shape)
    return pl.BlockSpec((None,) + tuple(shape), lambda *_: (l,) + (0,) * nd,
                        pipeline_mode=pl.Buffered(1))


def _params(*sem):
    return pltpu.CompilerParams(dimension_semantics=sem, vmem_limit_bytes=VMEM_LIMIT)


def _prep_kernel(x_ref, g_ref, wc_ref, qn_ref, kvn_ref, wqa_ref, wqb_ref, wka_ref, wkr_ref,
                 wkb_ref, wvt_ref, vone_ref, cos_ref, sin_ref, q_ref, k_ref, vt_ref):
    h = _rms(x_ref[...], g_ref[...]).astype(BF16)
    c = _dot(h, wc_ref[...])
    cq = _rms(c[:, :Q_LORA], qn_ref[...]).astype(BF16)
    ckv = _rms(c[:, Q_LORA:Q_LORA + KV_LORA], kvn_ref[...]).astype(BF16)
    kr = c[:, Q_LORA + KV_LORA:].astype(BF16)
    cos = cos_ref[...]
    sin = sin_ref[...]
    scale = (QK_NOPE + QK_ROPE) ** -0.5 * LOG2_E
    qa = _dot(cq, wqa_ref[...])
    qb = _dot(cq, wqb_ref[...])
    ka = _dot(ckv, wka_ref[...]) + _dot(kr, wkr_ref[...])
    kb = _dot(kr, wkb_ref[...])
    for hd in range(MLA_HEADS):
        sl = slice(hd * HEAD_PAD, (hd + 1) * HEAD_PAD)
        q_ref[:, sl] = ((qa[:, sl] * cos + qb[:, sl] * sin) * scale).astype(BF16)
        k_ref[:, sl] = (ka[:, sl] * cos + kb[:, sl] * sin).astype(BF16)
    vt = lax.dot_general(wvt_ref[...], ckv, _NT, preferred_element_type=F32)
    vt_ref[...] = (vt + vone_ref[...]).astype(BF16)


def _prep(x2, sw, l, cos_t, sin_t):
    m = x2.shape[0]
    hp = MLA_HEADS * HEAD_PAD
    row = lambda i: (i, 0)
    slab = jax.ShapeDtypeStruct((m, hp), BF16)
    return pl.pallas_call(
        _prep_kernel,
        grid=(m // TM_PREP,),
        in_specs=[
            pl.BlockSpec((TM_PREP, D_MODEL), row),
            _layer((1, D_MODEL), l),
            _layer((D_MODEL, N_LATENT), l),
            _layer((1, Q_LORA), l),
            _layer((1, KV_LORA), l),
            _layer((Q_LORA, hp), l),
            _layer((Q_LORA, hp), l),
            _layer((KV_LORA, hp), l),
            _resident((LANES, hp)),
            _resident((LANES, hp)),
            _layer((hp, KV_LORA), l),
            _resident((hp, 1)),
            pl.BlockSpec((TM_PREP, LANES), row),
            pl.BlockSpec((TM_PREP, LANES), row),
        ],
        out_specs=[pl.BlockSpec((TM_PREP, hp), row), pl.BlockSpec((TM_PREP, hp), row),
                   pl.BlockSpec((hp, TM_PREP), lambda i: (0, i))],
        out_shape=[slab, slab, jax.ShapeDtypeStruct((hp, m), BF16)],
        compiler_params=_params("parallel"),
    )(x2, sw["norm_mix_pre"], sw["w_c"], sw["q_norm"], sw["kv_norm"], sw["wq_a"], sw["wq_b"],
      sw["wk_a"], sw["wk_r"], sw["wk_b"], sw["wv_t"], sw["v_one"], cos_t, sin_t)


def _attn_kernel(q_ref, k_ref, vt_ref, o_ref, *bufs):
    seq = k_ref.shape[0]
    n_kc = seq // KC_ATTN
    n_qb = seq // QB_ATTN
    heads = [slice(hh * HEAD_PAD, (hh + 1) * HEAD_PAD) for hh in range(2)]

    def scores_chunk(blk, s_ref, c, run_max):
        keys = slice(c * KC_ATTN, (c + 1) * KC_ATTN)
        q_rows = slice(blk * QB_ATTN, (blk + 1) * QB_ATTN)
        for hh, sl in enumerate(heads):
            st = lax.dot_general(k_ref[keys, sl], q_ref[q_rows, sl], _NT,
                                 preferred_element_type=F32)
            s_ref[hh, keys, :] = st
            cm = run_max[hh]
            for r in range(KC_ATTN // SUBLANES):
                part = st[r * SUBLANES:(r + 1) * SUBLANES, :]
                cm = part if cm is None else jnp.maximum(cm, part)
            run_max[hh] = cm

    def values_chunk(s_ref, c, row_max, acc):
        keys = slice(c * KC_ATTN, (c + 1) * KC_ATTN)
        for hh, sl in enumerate(heads):
            pt = jnp.exp2(s_ref[hh, keys, :] - row_max[hh]).astype(BF16)
            t = _dot(vt_ref[sl, keys], pt)
            acc[hh] = t if acc[hh] is None else acc[hh] + t

    ahead = len(bufs) - 1
    maxes = {}
    for blk in range(min(ahead, n_qb)):
        maxes[blk] = [None, None]
        for c in range(n_kc):
            scores_chunk(blk, bufs[blk % len(bufs)], c, maxes[blk])
    for blk in range(n_qb):
        row_max = [jnp.max(mx, axis=0, keepdims=True) for mx in maxes.pop(blk)]
        acc = [None, None]
        nxt = blk + ahead
        if nxt < n_qb:
            maxes[nxt] = [None, None]
        for c in range(n_kc):
            if nxt < n_qb:
                scores_chunk(nxt, bufs[nxt % len(bufs)], c, maxes[nxt])
            values_chunk(bufs[blk % len(bufs)], c, row_max, acc)
        halves = [(a * (1.0 / a[V_HEAD:V_HEAD + 1, :]))[:V_HEAD, :] for a in acc]
        o_ref[blk * QB_ATTN:(blk + 1) * QB_ATTN, :] = jnp.concatenate(halves, axis=0).T.astype(BF16)


def _attention(q, k, vt, batch, seq):
    m = q.shape[0]
    pair = 2 * HEAD_PAD
    return pl.pallas_call(
        _attn_kernel,
        grid=(batch, MLA_HEADS // 2),
        in_specs=[
            pl.BlockSpec((seq, pair), lambda b, p: (b, p)),
            pl.BlockSpec((seq, pair), lambda b, p: (b, p)),
            pl.BlockSpec((pair, seq), lambda b, p: (p, b)),
        ],
        out_specs=pl.BlockSpec((seq, 2 * V_HEAD), lambda b, p: (b, p)),
        out_shape=jax.ShapeDtypeStruct((m, MLA_HEADS * V_HEAD), BF16),
        scratch_shapes=[
            pltpu.VMEM((2, seq, QB_ATTN), F32),
            pltpu.VMEM((2, seq, QB_ATTN), F32),
            pltpu.VMEM((2, seq, QB_ATTN), F32),
        ],
        compiler_params=_params("parallel", "parallel"),
    )(q, k, vt)


def _depthwise_rows(z_ref, w_ref, bias, out_ref, r0, ls, ntaps):
    nvo = CONV_ROWS // SUBLANES
    shifts = [HALO - ntaps // 2 + k for k in range(ntaps)]
    lo = min(shifts) // SUBLANES
    hi = (max(shifts) + SUBLANES - 1) // SUBLANES + nvo - 1
    win = {i: z_ref[pl.ds(r0 + i * SUBLANES, SUBLANES), ls] for i in range(lo, hi + 1)}
    sub = lax.broadcasted_iota(jnp.int32, (SUBLANES, LANES), 0)
    acc = [bias] * nvo
    for r in range(SUBLANES):
        taps = [k for k in range(ntaps) if shifts[k] % SUBLANES == r]
        if not taps:
            continue
        need = sorted({shifts[k] // SUBLANES + j for k in taps for j in range(nvo)})
        if r == 0:
            sh = {i: win[i] for i in need}
        else:
            rolled = {i: pltpu.roll(win[i], SUBLANES - r, 0)
                      for i in sorted(set(need) | {i + 1 for i in need})}
            sh = {i: jnp.where(sub < SUBLANES - r, rolled[i], rolled[i + 1]) for i in need}
        for k in taps:
            a = shifts[k] // SUBLANES
            wv = w_ref[k, :, ls]
            for j in range(nvo):
                t = sh[a + j] * wv
                acc[j] = t if acc[j] is None else acc[j] + t
    for j in range(nvo):
        out_ref[pl.ds(r0 + j * SUBLANES, SUBLANES), ls] = acc[j]


def _mix_kernel(x_ref, xp_ref, xn_ref, att_ref, gpre_ref, wm_ref, woa_ref, wob_ref, woc_ref,
                wod_ref, wout_ref, scw_ref, cfw_ref, cfb_ref, cflg_ref, cflb_ref, gmlg_ref,
                gmlb_ref, gmws_ref, gmbs_ref, gateb_ref, npost_ref, out_ref,
                h_ref, zc_ref, zs_ref, yb_ref, yc_ref, yd_ref):
    tm = TM_MIX
    i = pl.program_id(1)
    gpre = gpre_ref[...]
    h_ref[0:HALO, :] = _rms(xp_ref[...], gpre).astype(BF16)
    h_ref[HALO:HALO + tm, :] = _rms(x_ref[...], gpre).astype(BF16)
    h_ref[HALO + tm:, :] = _rms(xn_ref[...], gpre).astype(BF16)
    tile = slice(HALO, HALO + tm)

    def proj_ext(col, width):
        return _dot(h_ref[...], wm_ref[:, col:col + width])

    def proj(col, width):
        return _dot(h_ref[tile, :], wm_ref[:, col:col + width])

    row = lax.broadcasted_iota(jnp.int32, (tm + 2 * HALO, 1), 0)
    keep = jnp.logical_and(jnp.logical_or(row >= HALO, i > 0),
                           jnp.logical_or(row < HALO + tm, i < pl.num_programs(1) - 1))
    zc = proj_ext(COL_CF_A, CF_WIDTH) * _sigmoid(proj_ext(COL_CF_G, CF_WIDTH))
    zc_ref[...] = jnp.where(keep, zc, 0.0)
    zs = proj_ext(COL_SC_C, SC_WIDTH) * proj_ext(COL_SC_X, SC_WIDTH)
    zs_ref[...] = jnp.where(keep, zs, 0.0)

    for c in range(tm // CONV_ROWS):
        for lt in range(CF_WIDTH // LANES):
            ls = slice(lt * LANES, (lt + 1) * LANES)
            _depthwise_rows(zc_ref, cfw_ref, cfb_ref[:, ls], yc_ref, c * CONV_ROWS, ls, CF_K)
            _depthwise_rows(zs_ref, scw_ref, None, yb_ref, c * CONV_ROWS, ls, SC_K)
    yc = _layer_norm(yc_ref[...], cflg_ref[...], cflb_ref[...])
    yc = (yc * _sigmoid(yc)).astype(BF16)
    yb = (proj(COL_SC_B, SC_WIDTH) * yb_ref[...]).astype(BF16)

    u = jax.nn.gelu(proj(COL_GM_U, GM_WIDTH))
    v = jax.nn.gelu(proj(COL_GM_V, GM_WIDTH))
    v = _layer_norm(v, gmlg_ref[...], gmlb_ref[...]).astype(BF16)
    gw = GM_WIDTH // GM_GROUPS
    for n in range(tm // GM_CHUNK):
        rs = slice(n * GM_CHUNK, (n + 1) * GM_CHUNK)
        for g in range(GM_GROUPS):
            cs = slice(g * gw, (g + 1) * gw)
            mixed = _dot(gmws_ref[g], v[rs, cs]) + gmbs_ref[:, g:g + 1]
            yd_ref[rs, cs] = (u[rs, cs] * mixed).astype(BF16)

    branches = ((att_ref[...], woa_ref), (yb, wob_ref), (yc, woc_ref), (yd_ref[...], wod_ref))
    mo = None
    for n in range(D_MODEL // MERGE_COLS):
        cs = slice(n * MERGE_COLS, (n + 1) * MERGE_COLS)
        merged = None
        for j, (act, w_ref) in enumerate(branches):
            gl = proj(COL_GATE + j * D_MODEL + n * MERGE_COLS, MERGE_COLS)
            t = _sigmoid(gl + gateb_ref[j:j + 1, cs]) * _dot(act, w_ref[:, cs])
            merged = t if merged is None else merged + t
        t = _dot(merged.astype(BF16), wout_ref[cs, :])
        mo = t if mo is None else mo + t
    out_ref[...] = x_ref[...] + _rms(mo, npost_ref[...])


def _mix(x2, att, sw, l, batch, seq):
    m = x2.shape[0]
    nt = seq // TM_MIX
    hb = TM_MIX // HALO
    last = m // HALO - 1
    tile = lambda b, i: (b * nt + i, 0)
    prev = lambda b, i: (jnp.maximum((b * nt + i) * hb - 1, 0), 0)
    nxt = lambda b, i: (jnp.minimum((b * nt + i + 1) * hb, last), 0)
    return pl.pallas_call(
        _mix_kernel,
        grid=(batch, nt),
        in_specs=[
            pl.BlockSpec((TM_MIX, D_MODEL), tile),
            pl.BlockSpec((HALO, D_MODEL), prev),
            pl.BlockSpec((HALO, D_MODEL), nxt),
            pl.BlockSpec((TM_MIX, MLA_HEADS * V_HEAD), tile),
            _layer((1, D_MODEL), l),
            _layer((D_MODEL, N_MAIN), l),
            _layer((MLA_HEADS * V_HEAD, D_MODEL), l),
            _layer((SC_WIDTH, D_MODEL), l),
            _layer((CF_WIDTH, D_MODEL), l),
            _layer((GM_WIDTH, D_MODEL), l),
            _layer((D_MODEL, D_MODEL), l),
            _layer((SC_K, SUBLANES, SC_WIDTH), l),
            _layer((CF_K, SUBLANES, CF_WIDTH), l),
            _layer((SUBLANES, CF_WIDTH), l),
            _layer((1, CF_WIDTH), l),
            _layer((1, CF_WIDTH), l),
            _layer((1, GM_WIDTH), l),
            _layer((1, GM_WIDTH), l),
            _layer((GM_GROUPS, GM_CHUNK, GM_CHUNK), l),
            _layer((GM_CHUNK, GM_GROUPS), l),
            _layer((N_BRANCH, D_MODEL), l),
            _layer((1, D_MODEL), l),
        ],
        out_specs=pl.BlockSpec((TM_MIX, D_MODEL), tile),
        out_shape=jax.ShapeDtypeStruct((m, D_MODEL), F32),
        scratch_shapes=[
            pltpu.VMEM((TM_MIX + 2 * HALO, D_MODEL), BF16),
            pltpu.VMEM((TM_MIX + 2 * HALO, CF_WIDTH), F32),
            pltpu.VMEM((TM_MIX + 2 * HALO, SC_WIDTH), F32),
            pltpu.VMEM((TM_MIX, SC_WIDTH), F32),
            pltpu.VMEM((TM_MIX, CF_WIDTH), F32),
            pltpu.VMEM((TM_MIX, GM_WIDTH), BF16),
        ],
        compiler_params=_params("parallel", "arbitrary"),
    )(x2, x2, x2, att, sw["norm_mix_pre"], sw["w_main"], sw["w_o_mla"], sw["w_o_sc"], sw["w_o_cf"],
      sw["w_o_gm"], sw["w_out"], sw["sc_conv_w"], sw["cf_conv_w"], sw["cf_conv_b"], sw["cf_ln_g"],
      sw["cf_ln_b"], sw["gm_ln_g"], sw["gm_ln_b"], sw["gm_ws"], sw["gm_bs_t"], sw["gate_b"],
      sw["norm_mix_post"])


def _ffn_kernel(x_ref, gpre_ref, wi_ref, wo_ref, gpost_ref, out_ref):
    x = x_ref[...]
    h = _rms(x, gpre_ref[...]).astype(BF16)
    acc = None
    for c in range(FFN_HIDDEN // FFN_CHUNK):
        cs = slice(c * FFN_CHUNK, (c + 1) * FFN_CHUNK)
        fg = _dot(h, wi_ref[:, cs])
        fu = _dot(h, wi_ref[:, FFN_HIDDEN + c * FFN_CHUNK:FFN_HIDDEN + (c + 1) * FFN_CHUNK])
        f = (fg * _sigmoid(fg) * fu).astype(BF16)
        t = _dot(f, wo_ref[cs, :])
        acc = t if acc is None else acc + t
    out_ref[...] = x + _rms(acc, gpost_ref[...])


def _ffn(x2, sw, l):
    m = x2.shape[0]
    row = lambda i: (i, 0)
    return pl.pallas_call(
        _ffn_kernel,
        grid=(m // TM_FFN,),
        in_specs=[
            pl.BlockSpec((TM_FFN, D_MODEL), row),
            _layer((1, D_MODEL), l),
            _layer((D_MODEL, 2 * FFN_HIDDEN), l),
            _layer((FFN_HIDDEN, D_MODEL), l),
            _layer((1, D_MODEL), l),
        ],
        out_specs=pl.BlockSpec((TM_FFN, D_MODEL), row),
        out_shape=jax.ShapeDtypeStruct((m, D_MODEL), F32),
        compiler_params=_params("parallel"),
    )(x2, sw["norm_ffn_pre"], sw["w_ffn_in"], sw["w_ffn_out"], sw["norm_ffn_post"])


def _head_slab(w, src_cols, dst_off, width, sign=1.0):
    depth, rows = w.shape[:2]
    out = jnp.zeros((depth, rows, MLA_HEADS, HEAD_PAD), F32)
    src = w.reshape(depth, rows, MLA_HEADS, -1)[..., src_cols:src_cols + width]
    return out.at[..., dst_off:dst_off + width].set(sign * src)


def _rope_place():
    half = QK_ROPE // 2
    plain = np.zeros((LANES, MLA_HEADS, HEAD_PAD), np.float32)
    rot = np.zeros((LANES, MLA_HEADS, HEAD_PAD), np.float32)
    for j in range(QK_ROPE):
        plain[j, :, QK_NOPE + j] = 1.0
    for j in range(half):
        rot[half + j, :, QK_NOPE + j] = -1.0
        rot[j, :, QK_NOPE + half + j] = 1.0
    shape = (LANES, MLA_HEADS * HEAD_PAD)
    return jnp.asarray(plain.reshape(shape), BF16), jnp.asarray(rot.reshape(shape), BF16)


def _stacked_weights(p):
    hp = MLA_HEADS * HEAD_PAD
    half = QK_ROPE // 2
    w_in = p["w_in"]
    depth = w_in.shape[0]
    o = Q_LORA + KV_LORA + QK_ROPE

    def cols(k):
        return w_in[:, :, o + k * 512:o + (k + 1) * 512]

    sc_b, sc_c, sc_x, cf_a, cf_g, gm_u, gm_v = (cols(k) for k in range(7))
    gates = w_in[:, :, o + 7 * 512:]
    w_main = jnp.concatenate([cf_a, cf_g, sc_c, sc_x, sc_b, gm_u, gm_v, gates], axis=2).astype(BF16)
    w_c = jnp.concatenate([w_in[:, :, :o], jnp.zeros((depth, D_MODEL, N_LATENT - o), F32)],
                          axis=2).astype(BF16)

    w_uq = p["w_uq"]
    wq_a = _head_slab(w_uq, 0, 0, QK_NOPE + QK_ROPE)
    wq_b = (_head_slab(w_uq, QK_NOPE + half, QK_NOPE, half, -1.0)
            + _head_slab(w_uq, QK_NOPE, QK_NOPE + half, half))
    w_ukv = p["w_ukv"]
    wk_a = _head_slab(w_ukv, 0, 0, QK_NOPE)
    wv = _head_slab(w_ukv, QK_NOPE, 0, V_HEAD)
    wk_r, wk_b = _rope_place()
    v_one = np.zeros((MLA_HEADS, HEAD_PAD, 1), np.float32)
    v_one[:, V_HEAD] = 1.0

    def row(a):
        return a[:, None, :]

    def rows8(a):
        return jnp.broadcast_to(a[..., None, :], a.shape[:-1] + (SUBLANES, a.shape[-1]))

    return {
        "norm_mix_pre": row(p["norm_mix_pre"]),
        "w_main": w_main,
        "w_c": w_c,
        "q_norm": row(p["mla_q_norm"]),
        "kv_norm": row(p["mla_kv_norm"]),
        "wq_a": wq_a.reshape(depth, Q_LORA, hp).astype(BF16),
        "wq_b": wq_b.reshape(depth, Q_LORA, hp).astype(BF16),
        "wk_a": wk_a.reshape(depth, KV_LORA, hp).astype(BF16),
        "wk_r": wk_r,
        "wk_b": wk_b,
        "wv_t": jnp.swapaxes(wv.reshape(depth, KV_LORA, hp), 1, 2).astype(BF16),
        "v_one": jnp.asarray(v_one.reshape(hp, 1)),
        "w_o_mla": p["w_o_mla"].astype(BF16),
        "w_o_sc": p["w_o_sc"].astype(BF16),
        "w_o_cf": p["w_o_cf"].astype(BF16),
        "w_o_gm": p["w_o_gm"].astype(BF16),
        "w_out": p["w_out"].astype(BF16),
        "sc_conv_w": rows8(p["sc_conv_w"]),
        "cf_conv_w": rows8(p["cf_conv_w"]),
        "cf_conv_b": rows8(p["cf_conv_b"]),
        "cf_ln_g": row(p["cf_ln_g"]),
        "cf_ln_b": row(p["cf_ln_b"]),
        "gm_ln_g": row(p["gm_ln_g"]),
        "gm_ln_b": row(p["gm_ln_b"]),
        "gm_ws": p["gm_ws"].astype(BF16),
        "gm_bs_t": jnp.swapaxes(p["gm_bs"], 1, 2),
        "gate_b": p["gate_b"],
        "norm_mix_post": row(p["norm_mix_post"]),
        "norm_ffn_pre": row(p["norm_ffn_pre"]),
        "w_ffn_in": p["w_ffn_in"].astype(BF16),
        "w_ffn_out": p["w_ffn_out"].astype(BF16),
        "norm_ffn_post": row(p["norm_ffn_post"]),
    }


def _rope_tables(positions):
    m = positions.size
    inv_freq = ROPE_THETA ** (-jnp.arange(0, QK_ROPE, 2, dtype=F32) / QK_ROPE)
    ang = positions.astype(F32).reshape(m, 1) * inv_freq
    cos, sin = jnp.cos(ang), jnp.sin(ang)
    pad = jnp.zeros((m, HEAD_PAD - QK_NOPE - QK_ROPE), F32)
    cos_t = jnp.concatenate([jnp.ones((m, QK_NOPE), F32), cos, cos, pad], axis=1)
    sin_t = jnp.concatenate([jnp.zeros((m, QK_NOPE), F32), sin, sin, pad], axis=1)
    return cos_t, sin_t


def kernel(x, positions, norm_mix_pre, w_in, mla_q_norm, w_uq, mla_kv_norm, w_ukv, w_o_mla, sc_conv_w, w_o_sc, cf_conv_w, cf_conv_b, cf_ln_g, cf_ln_b, w_o_cf, gm_ln_g, gm_ln_b, gm_ws, gm_bs, w_o_gm, gate_b, w_out, norm_mix_post, norm_ffn_pre, w_ffn_in, w_ffn_out, norm_ffn_post):
    p = dict(norm_mix_pre=norm_mix_pre, w_in=w_in, mla_q_norm=mla_q_norm, w_uq=w_uq,
             mla_kv_norm=mla_kv_norm, w_ukv=w_ukv, w_o_mla=w_o_mla, sc_conv_w=sc_conv_w,
             w_o_sc=w_o_sc, cf_conv_w=cf_conv_w, cf_conv_b=cf_conv_b, cf_ln_g=cf_ln_g,
             cf_ln_b=cf_ln_b, w_o_cf=w_o_cf, gm_ln_g=gm_ln_g, gm_ln_b=gm_ln_b, gm_ws=gm_ws,
             gm_bs=gm_bs, w_o_gm=w_o_gm, gate_b=gate_b, w_out=w_out, norm_mix_post=norm_mix_post,
             norm_ffn_pre=norm_ffn_pre, w_ffn_in=w_ffn_in, w_ffn_out=w_ffn_out,
             norm_ffn_post=norm_ffn_post)
    batch, seq, d = x.shape
    depth = w_in.shape[0]
    m = batch * seq
    assert d == D_MODEL and m % TM_PREP == 0 and m % TM_FFN == 0
    assert seq % QB_ATTN == 0 and seq % KC_ATTN == 0 and seq % TM_MIX == 0
    cos_t, sin_t = _rope_tables(positions)
    x2 = x.reshape(m, d)
    sw = _stacked_weights(p)
    for l in range(depth):
        q, k, vt = _prep(x2, sw, l, cos_t, sin_t)
        att = _attention(q, k, vt, batch, seq)
        x2 = _mix(x2, att, sw, l, batch, seq)
        x2 = _ffn(x2, sw, l)
    return x2.reshape(batch, seq, d)
```

```python
import jax
import jax.numpy as jnp
import numpy as np
from jax import lax
from jax.experimental import pallas as pl
from jax.experimental.pallas import tpu as pltpu

F32 = jnp.float32
BF16 = jnp.bfloat16

D_MODEL = 1024
MLA_HEADS = 8
QK_NOPE = 64
QK_ROPE = 32
V_HEAD = 64
Q_LORA = 256
KV_LORA = 256
ROPE_THETA = 10000.0
SC_WIDTH = 512
SC_K = 3
CF_WIDTH = 512
CF_K = 31
GM_GROUPS = 4
GM_WIDTH = 512
GM_CHUNK = 128
N_BRANCH = 4
FFN_HIDDEN = 2816
EPS = 1e-6

LANES = 128
HEAD_PAD = LANES
HALO = 16
SUBLANES = 8
CONV_ROWS = 64

N_LATENT = 640
COL_SC_B = N_LATENT
COL_SC_C = COL_SC_B + SC_WIDTH
COL_SC_X = COL_SC_C + SC_WIDTH
COL_CF_A = COL_SC_X + SC_WIDTH
COL_CF_G = COL_CF_A + CF_WIDTH
COL_GM_U = COL_CF_G + CF_WIDTH
COL_GM_V = COL_GM_U + GM_WIDTH
COL_GATE = COL_GM_V + GM_WIDTH
N_IN_PAD = COL_GATE + N_BRANCH * D_MODEL

TM_PREP = 512
QB_ATTN = 256
KC_ATTN = 256
TM_MIX = 512
MERGE_COLS = 256
TM_FFN = 512
FFN_CHUNK = 1408
VMEM_LIMIT = 56 * 1024 * 1024


def _rms(x, g):
    ms = jnp.mean(x * x, axis=-1, keepdims=True)
    return x * lax.rsqrt(ms + EPS) * g


def _layer_norm(x, g, b):
    mu = jnp.mean(x, axis=-1, keepdims=True)
    xc = x - mu
    var = jnp.mean(xc * xc, axis=-1, keepdims=True)
    return xc * lax.rsqrt(var + EPS) * g + b


def _sigmoid(x):
    return 0.5 * jnp.tanh(0.5 * x) + 0.5


def _dot(a, b):
    return jnp.dot(a, b, preferred_element_type=F32)


_NT = (((1,), (1,)), ((), ()))
LOG2_E = 1.4426950408889634


def _resident(shape):
    nd = len(shape)
    return pl.BlockSpec(shape, lambda *_: (0,) * nd, pipeline_mode=pl.Buffered(1))


def _layer(shape, l):
    nd = len(shape)
    return pl.BlockSpec((None,) + tuple(shape), lambda *_: (l,) + (0,) * nd,
                        pipeline_mode=pl.Buffered(1))


def _params(*sem):
    return pltpu.CompilerParams(dimension_semantics=sem, vmem_limit_bytes=VMEM_LIMIT)


def _prep_kernel(x_ref, g_ref, wc_ref, qn_ref, kvn_ref, wq_ref, wka_ref, wkr_ref,
                 wkb_ref, wvt_ref, vone_ref, cos_ref, sin_ref, cost_ref, sint_ref,
                 qt_ref, k_ref, vt_ref):
    h = _rms(x_ref[...], g_ref[...]).astype(BF16)
    c = _dot(h, wc_ref[...])
    cq = _rms(c[:, :Q_LORA], qn_ref[...]).astype(BF16)
    ckv = _rms(c[:, Q_LORA:Q_LORA + KV_LORA], kvn_ref[...]).astype(BF16)
    kr = c[:, Q_LORA + KV_LORA:].astype(BF16)
    cos = cos_ref[...]
    sin = sin_ref[...]
    scale = (QK_NOPE + QK_ROPE) ** -0.5 * LOG2_E
    qt = lax.dot_general(wq_ref[...], cq, _NT, preferred_element_type=F32)
    ka = _dot(ckv, wka_ref[...]) + _dot(kr, wkr_ref[...])
    kb = _dot(kr, wkb_ref[...])
    cos_s = cost_ref[...] * scale
    sin_s = sint_ref[...] * scale
    half = QK_ROPE // 2
    for hd in range(MLA_HEADS):
        sl = slice(hd * HEAD_PAD, (hd + 1) * HEAD_PAD)
        qh = qt[sl, :]
        swapped = jnp.concatenate([qh[:QK_NOPE], qh[QK_NOPE + half:QK_NOPE + QK_ROPE],
                                   qh[QK_NOPE:QK_NOPE + half], qh[QK_NOPE + QK_ROPE:]], axis=0)
        qt_ref[sl, :] = (qh * cos_s + swapped * sin_s).astype(BF16)
        k_ref[:, sl] = (ka[:, sl] * cos + kb[:, sl] * sin).astype(BF16)
    vt = lax.dot_general(wvt_ref[...], ckv, _NT, preferred_element_type=F32)
    vt_ref[...] = (vt + vone_ref[...]).astype(BF16)


def _prep(x2, sw, l, rope):
    m = x2.shape[0]
    hp = MLA_HEADS * HEAD_PAD
    row = lambda i: (i, 0)
    col = lambda i: (0, i)
    slab_t = jax.ShapeDtypeStruct((hp, m), BF16)
    return pl.pallas_call(
        _prep_kernel,
        grid=(m // TM_PREP,),
        in_specs=[
            pl.BlockSpec((TM_PREP, D_MODEL), row),
            _layer((1, D_MODEL), l),
            pl.BlockSpec((None, D_MODEL, N_LATENT), lambda i: (l, 0, 0),
                         pipeline_mode=pl.Buffered(1)),
            _layer((1, Q_LORA), l),
            _layer((1, KV_LORA), l),
            _layer((hp, Q_LORA), l),
            _layer((KV_LORA, hp), l),
            _resident((LANES, hp)),
            _resident((LANES, hp)),
            _layer((hp, KV_LORA), l),
            _resident((hp, 1)),
            pl.BlockSpec((TM_PREP, LANES), row),
            pl.BlockSpec((TM_PREP, LANES), row),
            pl.BlockSpec((LANES, TM_PREP), col),
            pl.BlockSpec((LANES, TM_PREP), col),
        ],
        out_specs=[pl.BlockSpec((hp, TM_PREP), col), pl.BlockSpec((TM_PREP, hp), row),
                   pl.BlockSpec((hp, TM_PREP), col)],
        out_shape=[slab_t, jax.ShapeDtypeStruct((m, hp), BF16), slab_t],
        compiler_params=_params("parallel"),
    )(x2, sw["norm_mix_pre"], sw["w_in"], sw["q_norm"], sw["kv_norm"], sw["wq_t"],
      sw["wk_a"], sw["wk_r"], sw["wk_b"], sw["wv_t"], sw["v_one"], *rope)


def _attn_kernel(qt_ref, k_ref, vt_ref, o_ref, *bufs):
    seq = k_ref.shape[0]
    n_kc = seq // KC_ATTN
    n_qb = seq // QB_ATTN
    heads = [slice(hh * HEAD_PAD, (hh + 1) * HEAD_PAD) for hh in range(2)]

    def scores_chunk(blk, s_ref, c, run_max):
        keys = slice(c * KC_ATTN, (c + 1) * KC_ATTN)
        q_cols = slice(blk * QB_ATTN, (blk + 1) * QB_ATTN)
        for hh, sl in enumerate(heads):
            st = _dot(k_ref[keys, sl], qt_ref[sl, q_cols])
            s_ref[hh, keys, :] = st
            cm = run_max[hh]
            for r in range(KC_ATTN // SUBLANES):
                part = st[r * SUBLANES:(r + 1) * SUBLANES, :]
                cm = part if cm is None else jnp.maximum(cm, part)
            run_max[hh] = cm

    def values_chunk(s_ref, c, row_max, acc):
        keys = slice(c * KC_ATTN, (c + 1) * KC_ATTN)
        for hh, sl in enumerate(heads):
            pt = jnp.exp2(s_ref[hh, keys, :] - row_max[hh]).astype(BF16)
            t = _dot(vt_ref[sl, keys], pt)
            acc[hh] = t if acc[hh] is None else acc[hh] + t

    ahead = len(bufs) - 1
    maxes = {}
    for blk in range(min(ahead, n_qb)):
        maxes[blk] = [None, None]
        for c in range(n_kc):
            scores_chunk(blk, bufs[blk % len(bufs)], c, maxes[blk])
    for blk in range(n_qb):
        row_max = [jnp.max(mx, axis=0, keepdims=True) for mx in maxes.pop(blk)]
        acc = [None, None]
        nxt = blk + ahead
        if nxt < n_qb:
            maxes[nxt] = [None, None]
        for c in range(n_kc):
            if nxt < n_qb:
                scores_chunk(nxt, bufs[nxt % len(bufs)], c, maxes[nxt])
            values_chunk(bufs[blk % len(bufs)], c, row_max, acc)
        halves = [(a * (1.0 / a[V_HEAD:V_HEAD + 1, :]))[:V_HEAD, :] for a in acc]
        o_ref[blk * QB_ATTN:(blk + 1) * QB_ATTN, :] = jnp.concatenate(halves, axis=0).T.astype(BF16)


def _attention(qt, k, vt, batch, seq):
    m = k.shape[0]
    pair = 2 * HEAD_PAD
    return pl.pallas_call(
        _attn_kernel,
        grid=(batch, MLA_HEADS // 2),
        in_specs=[
            pl.BlockSpec((pair, seq), lambda b, p: (p, b)),
            pl.BlockSpec((seq, pair), lambda b, p: (b, p)),
            pl.BlockSpec((pair, seq), lambda b, p: (p, b)),
        ],
        out_specs=pl.BlockSpec((seq, 2 * V_HEAD), lambda b, p: (b, p)),
        out_shape=jax.ShapeDtypeStruct((m, MLA_HEADS * V_HEAD), BF16),
        scratch_shapes=[
            pltpu.VMEM((2, seq, QB_ATTN), F32),
            pltpu.VMEM((2, seq, QB_ATTN), F32),
            pltpu.VMEM((2, seq, QB_ATTN), F32),
        ],
        compiler_params=_params("parallel", "parallel"),
    )(qt, k, vt)


def _depthwise_rows(z_ref, w_ref, bias, out_ref, r0, ls, ntaps):
    nvo = CONV_ROWS // SUBLANES
    shifts = [HALO - ntaps // 2 + k for k in range(ntaps)]
    lo = min(shifts) // SUBLANES
    hi = (max(shifts) + SUBLANES - 1) // SUBLANES + nvo - 1
    win = {i: z_ref[pl.ds(r0 + i * SUBLANES, SUBLANES), ls] for i in range(lo, hi + 1)}
    sub = lax.broadcasted_iota(jnp.int32, (SUBLANES, LANES), 0)
    acc = [bias] * nvo
    for r in range(SUBLANES):
        taps = [k for k in range(ntaps) if shifts[k] % SUBLANES == r]
        if not taps:
            continue
        need = sorted({shifts[k] // SUBLANES + j for k in taps for j in range(nvo)})
        if r == 0:
            sh = {i: win[i] for i in need}
        else:
            rolled = {i: pltpu.roll(win[i], SUBLANES - r, 0)
                      for i in sorted(set(need) | {i + 1 for i in need})}
            sh = {i: jnp.where(sub < SUBLANES - r, rolled[i], rolled[i + 1]) for i in need}
        for k in taps:
            a = shifts[k] // SUBLANES
            wv = w_ref[k, :, ls]
            for j in range(nvo):
                t = sh[a + j] * wv
                acc[j] = t if acc[j] is None else acc[j] + t
    for j in range(nvo):
        out_ref[pl.ds(r0 + j * SUBLANES, SUBLANES), ls] = acc[j]


def _mix_kernel(x_ref, xp_ref, xn_ref, att_ref, gpre_ref, wm_ref, woa_ref, wob_ref, woc_ref,
                wod_ref, wout_ref, scw_ref, cfw_ref, cfb_ref, cflg_ref, cflb_ref, gmlg_ref,
                gmlb_ref, gmws_ref, gmbs_ref, gateb_ref, npost_ref, out_ref,
                h_ref, zc_ref, zs_ref, yb_ref, yc_ref, yd_ref):
    tm = TM_MIX
    i = pl.program_id(1)
    gpre = gpre_ref[...]
    h_ref[0:HALO, :] = _rms(xp_ref[...], gpre).astype(BF16)
    h_ref[HALO:HALO + tm, :] = _rms(x_ref[...], gpre).astype(BF16)
    h_ref[HALO + tm:, :] = _rms(xn_ref[...], gpre).astype(BF16)
    tile = slice(HALO, HALO + tm)

    def proj_ext(col, width):
        return _dot(h_ref[...], wm_ref[:, col:col + width])

    def proj(col, width):
        return _dot(h_ref[tile, :], wm_ref[:, col:col + width])

    row = lax.broadcasted_iota(jnp.int32, (tm + 2 * HALO, 1), 0)
    keep = jnp.logical_and(jnp.logical_or(row >= HALO, i > 0),
                           jnp.logical_or(row < HALO + tm, i < pl.num_programs(1) - 1))
    zc = proj_ext(COL_CF_A, CF_WIDTH) * _sigmoid(proj_ext(COL_CF_G, CF_WIDTH))
    zc_ref[...] = jnp.where(keep, zc, 0.0)
    zs = proj_ext(COL_SC_C, SC_WIDTH) * proj_ext(COL_SC_X, SC_WIDTH)
    zs_ref[...] = jnp.where(keep, zs, 0.0)

    for c in range(tm // CONV_ROWS):
        for lt in range(CF_WIDTH // LANES):
            ls = slice(lt * LANES, (lt + 1) * LANES)
            _depthwise_rows(zc_ref, cfw_ref, cfb_ref[:, ls], yc_ref, c * CONV_ROWS, ls, CF_K)
            _depthwise_rows(zs_ref, scw_ref, None, yb_ref, c * CONV_ROWS, ls, SC_K)
    yc = _layer_norm(yc_ref[...], cflg_ref[...], cflb_ref[...])
    yc = (yc * _sigmoid(yc)).astype(BF16)
    yb = (proj(COL_SC_B, SC_WIDTH) * yb_ref[...]).astype(BF16)

    u = jax.nn.gelu(proj(COL_GM_U, GM_WIDTH))
    v = jax.nn.gelu(proj(COL_GM_V, GM_WIDTH))
    v = _layer_norm(v, gmlg_ref[...], gmlb_ref[...]).astype(BF16)
    gw = GM_WIDTH // GM_GROUPS
    for n in range(tm // GM_CHUNK):
        rs = slice(n * GM_CHUNK, (n + 1) * GM_CHUNK)
        for g in range(GM_GROUPS):
            cs = slice(g * gw, (g + 1) * gw)
            mixed = _dot(gmws_ref[g], v[rs, cs]) + gmbs_ref[:, g:g + 1]
            yd_ref[rs, cs] = (u[rs, cs] * mixed).astype(BF16)

    branches = ((att_ref[...], woa_ref), (yb, wob_ref), (yc, woc_ref), (yd_ref[...], wod_ref))
    mo = None
    for n in range(D_MODEL // MERGE_COLS):
        cs = slice(n * MERGE_COLS, (n + 1) * MERGE_COLS)
        merged = None
        for j, (act, w_ref) in enumerate(branches):
            gl = proj(COL_GATE + j * D_MODEL + n * MERGE_COLS, MERGE_COLS)
            t = _sigmoid(gl + gateb_ref[j:j + 1, cs]) * _dot(act, w_ref[:, cs])
            merged = t if merged is None else merged + t
        t = _dot(merged.astype(BF16), wout_ref[cs, :])
        mo = t if mo is None else mo + t
    out_ref[...] = x_ref[...] + _rms(mo, npost_ref[...])


def _mix(x2, att, sw, l, batch, seq):
    m = x2.shape[0]
    nt = seq // TM_MIX
    hb = TM_MIX // HALO
    last = m // HALO - 1
    tile = lambda b, i: (b * nt + i, 0)
    prev = lambda b, i: (jnp.maximum((b * nt + i) * hb - 1, 0), 0)
    nxt = lambda b, i: (jnp.minimum((b * nt + i + 1) * hb, last), 0)
    return pl.pallas_call(
        _mix_kernel,
        grid=(batch, nt),
        in_specs=[
            pl.BlockSpec((TM_MIX, D_MODEL), tile),
            pl.BlockSpec((HALO, D_MODEL), prev),
            pl.BlockSpec((HALO, D_MODEL), nxt),
            pl.BlockSpec((TM_MIX, MLA_HEADS * V_HEAD), tile),
            _layer((1, D_MODEL), l),
            _layer((D_MODEL, N_IN_PAD), l),
            _layer((MLA_HEADS * V_HEAD, D_MODEL), l),
            _layer((SC_WIDTH, D_MODEL), l),
            _layer((CF_WIDTH, D_MODEL), l),
            _layer((GM_WIDTH, D_MODEL), l),
            _layer((D_MODEL, D_MODEL), l),
            _layer((SC_K, SUBLANES, SC_WIDTH), l),
            _layer((CF_K, SUBLANES, CF_WIDTH), l),
            _layer((SUBLANES, CF_WIDTH), l),
            _layer((1, CF_WIDTH), l),
            _layer((1, CF_WIDTH), l),
            _layer((1, GM_WIDTH), l),
            _layer((1, GM_WIDTH), l),
            _layer((GM_GROUPS, GM_CHUNK, GM_CHUNK), l),
            _layer((GM_CHUNK, GM_GROUPS), l),
            _layer((N_BRANCH, D_MODEL), l),
            _layer((1, D_MODEL), l),
        ],
        out_specs=pl.BlockSpec((TM_MIX, D_MODEL), tile),
        out_shape=jax.ShapeDtypeStruct((m, D_MODEL), F32),
        scratch_shapes=[
            pltpu.VMEM((TM_MIX + 2 * HALO, D_MODEL), BF16),
            pltpu.VMEM((TM_MIX + 2 * HALO, CF_WIDTH), F32),
            pltpu.VMEM((TM_MIX + 2 * HALO, SC_WIDTH), F32),
            pltpu.VMEM((TM_MIX, SC_WIDTH), F32),
            pltpu.VMEM((TM_MIX, CF_WIDTH), F32),
            pltpu.VMEM((TM_MIX, GM_WIDTH), BF16),
        ],
        compiler_params=_params("parallel", "arbitrary"),
    )(x2, x2, x2, att, sw["norm_mix_pre"], sw["w_in"], sw["w_o_mla"], sw["w_o_sc"], sw["w_o_cf"],
      sw["w_o_gm"], sw["w_out"], sw["sc_conv_w"], sw["cf_conv_w"], sw["cf_conv_b"], sw["cf_ln_g"],
      sw["cf_ln_b"], sw["gm_ln_g"], sw["gm_ln_b"], sw["gm_ws"], sw["gm_bs_t"], sw["gate_b"],
      sw["norm_mix_post"])


def _ffn_kernel(x_ref, gpre_ref, wi_ref, wo_ref, gpost_ref, out_ref):
    x = x_ref[...]
    h = _rms(x, gpre_ref[...]).astype(BF16)
    acc = None
    for c in range(FFN_HIDDEN // FFN_CHUNK):
        cs = slice(c * FFN_CHUNK, (c + 1) * FFN_CHUNK)
        fg = _dot(h, wi_ref[:, cs])
        fu = _dot(h, wi_ref[:, FFN_HIDDEN + c * FFN_CHUNK:FFN_HIDDEN + (c + 1) * FFN_CHUNK])
        f = (fg * _sigmoid(fg) * fu).astype(BF16)
        t = _dot(f, wo_ref[cs, :])
        acc = t if acc is None else acc + t
    out_ref[...] = x + _rms(acc, gpost_ref[...])


def _ffn(x2, sw, l):
    m = x2.shape[0]
    row = lambda i: (i, 0)
    return pl.pallas_call(
        _ffn_kernel,
        grid=(m // TM_FFN,),
        in_specs=[
            pl.BlockSpec((TM_FFN, D_MODEL), row),
            _layer((1, D_MODEL), l),
            _layer((D_MODEL, 2 * FFN_HIDDEN), l),
            _layer((FFN_HIDDEN, D_MODEL), l),
            _layer((1, D_MODEL), l),
        ],
        out_specs=pl.BlockSpec((TM_FFN, D_MODEL), row),
        out_shape=jax.ShapeDtypeStruct((m, D_MODEL), F32),
        compiler_params=_params("parallel"),
    )(x2, sw["norm_ffn_pre"], sw["w_ffn_in"], sw["w_ffn_out"], sw["norm_ffn_post"])


def _head_slab(w, src_cols, dst_off, width):
    depth, rows = w.shape[:2]
    out = jnp.zeros((depth, rows, MLA_HEADS, HEAD_PAD), F32)
    src = w.reshape(depth, rows, MLA_HEADS, -1)[..., src_cols:src_cols + width]
    return out.at[..., dst_off:dst_off + width].set(src)


def _rope_place():
    half = QK_ROPE // 2
    plain = np.zeros((LANES, MLA_HEADS, HEAD_PAD), np.float32)
    rot = np.zeros((LANES, MLA_HEADS, HEAD_PAD), np.float32)
    for j in range(QK_ROPE):
        plain[j, :, QK_NOPE + j] = 1.0
    for j in range(half):
        rot[half + j, :, QK_NOPE + j] = -1.0
        rot[j, :, QK_NOPE + half + j] = 1.0
    shape = (LANES, MLA_HEADS * HEAD_PAD)
    return jnp.asarray(plain.reshape(shape), BF16), jnp.asarray(rot.reshape(shape), BF16)


def _stacked_weights(p):
    hp = MLA_HEADS * HEAD_PAD
    w_in = p["w_in"]
    depth = w_in.shape[0]
    o = Q_LORA + KV_LORA + QK_ROPE
    w_in_pad = jnp.concatenate([w_in[:, :, :o].astype(BF16),
                                jnp.zeros((depth, D_MODEL, N_LATENT - o), BF16),
                                w_in[:, :, o:].astype(BF16)], axis=2)

    w_uq = p["w_uq"]
    wq = _head_slab(w_uq, 0, 0, QK_NOPE + QK_ROPE)
    w_ukv = p["w_ukv"]
    wk_a = _head_slab(w_ukv, 0, 0, QK_NOPE)
    wv = _head_slab(w_ukv, QK_NOPE, 0, V_HEAD)
    wk_r, wk_b = _rope_place()
    v_one = np.zeros((MLA_HEADS, HEAD_PAD, 1), np.float32)
    v_one[:, V_HEAD] = 1.0

    def row(a):
        return a[:, None, :]

    def rows8(a):
        return jnp.broadcast_to(a[..., None, :], a.shape[:-1] + (SUBLANES, a.shape[-1]))

    return {
        "norm_mix_pre": row(p["norm_mix_pre"]),
        "w_in": w_in_pad,
        "q_norm": row(p["mla_q_norm"]),
        "kv_norm": row(p["mla_kv_norm"]),
        "wq_t": jnp.swapaxes(wq.reshape(depth, Q_LORA, hp), 1, 2).astype(BF16),
        "wk_a": wk_a.reshape(depth, KV_LORA, hp).astype(BF16),
        "wk_r": wk_r,
        "wk_b": wk_b,
        "wv_t": jnp.swapaxes(wv.reshape(depth, KV_LORA, hp), 1, 2).astype(BF16),
        "v_one": jnp.asarray(v_one.reshape(hp, 1)),
        "w_o_mla": p["w_o_mla"].astype(BF16),
        "w_o_sc": p["w_o_sc"].astype(BF16),
        "w_o_cf": p["w_o_cf"].astype(BF16),
        "w_o_gm": p["w_o_gm"].astype(BF16),
        "w_out": p["w_out"].astype(BF16),
        "sc_conv_w": rows8(p["sc_conv_w"]),
        "cf_conv_w": rows8(p["cf_conv_w"]),
        "cf_conv_b": rows8(p["cf_conv_b"]),
        "cf_ln_g": row(p["cf_ln_g"]),
        "cf_ln_b": row(p["cf_ln_b"]),
        "gm_ln_g": row(p["gm_ln_g"]),
        "gm_ln_b": row(p["gm_ln_b"]),
        "gm_ws": p["gm_ws"].astype(BF16),
        "gm_bs_t": jnp.swapaxes(p["gm_bs"], 1, 2),
        "gate_b": p["gate_b"],
        "norm_mix_post": row(p["norm_mix_post"]),
        "norm_ffn_pre": row(p["norm_ffn_pre"]),
        "w_ffn_in": p["w_ffn_in"].astype(BF16),
        "w_ffn_out": p["w_ffn_out"].astype(BF16),
        "norm_ffn_post": row(p["norm_ffn_post"]),
    }


def _rope_tables(positions):
    m = positions.size
    inv_freq = ROPE_THETA ** (-jnp.arange(0, QK_ROPE, 2, dtype=F32) / QK_ROPE)
    ang = inv_freq[:, None] * positions.astype(F32).reshape(1, m)
    cos, sin = jnp.cos(ang), jnp.sin(ang)
    ones = jnp.ones((QK_NOPE, m), F32)
    zeros = jnp.zeros((QK_NOPE, m), F32)
    pad = jnp.zeros((HEAD_PAD - QK_NOPE - QK_ROPE, m), F32)
    cos_t = jnp.concatenate([ones, cos, cos, pad], axis=0)
    sin_t = jnp.concatenate([zeros, sin, sin, pad], axis=0)
    sin_q = jnp.concatenate([zeros, -sin, sin, pad], axis=0)
    return cos_t.T, sin_t.T, cos_t, sin_q


def kernel(x, positions, norm_mix_pre, w_in, mla_q_norm, w_uq, mla_kv_norm, w_ukv, w_o_mla, sc_conv_w, w_o_sc, cf_conv_w, cf_conv_b, cf_ln_g, cf_ln_b, w_o_cf, gm_ln_g, gm_ln_b, gm_ws, gm_bs, w_o_gm, gate_b, w_out, norm_mix_post, norm_ffn_pre, w_ffn_in, w_ffn_out, norm_ffn_post):
    p = dict(norm_mix_pre=norm_mix_pre, w_in=w_in, mla_q_norm=mla_q_norm, w_uq=w_uq,
             mla_kv_norm=mla_kv_norm, w_ukv=w_ukv, w_o_mla=w_o_mla, sc_conv_w=sc_conv_w,
             w_o_sc=w_o_sc, cf_conv_w=cf_conv_w, cf_conv_b=cf_conv_b, cf_ln_g=cf_ln_g,
             cf_ln_b=cf_ln_b, w_o_cf=w_o_cf, gm_ln_g=gm_ln_g, gm_ln_b=gm_ln_b, gm_ws=gm_ws,
             gm_bs=gm_bs, w_o_gm=w_o_gm, gate_b=gate_b, w_out=w_out, norm_mix_post=norm_mix_post,
             norm_ffn_pre=norm_ffn_pre, w_ffn_in=w_ffn_in, w_ffn_out=w_ffn_out,
             norm_ffn_post=norm_ffn_post)
    batch, seq, d = x.shape
    depth = w_in.shape[0]
    m = batch * seq
    assert d == D_MODEL and m % TM_PREP == 0 and m % TM_FFN == 0
    assert seq % QB_ATTN == 0 and seq % KC_ATTN == 0 and seq % TM_MIX == 0
    rope = _rope_tables(positions)
    x2 = x.reshape(m, d)
    sw = _stacked_weights(p)
    for l in range(depth):
        qt, k, vt = _prep(x2, sw, l, rope)
        att = _attention(qt, k, vt, batch, seq)
        x2 = _mix(x2, att, sw, l, batch, seq)
        x2 = _ffn(x2, sw, l)
    return x2.reshape(batch, seq, d)
```

```python
import jax
import jax.numpy as jnp
import numpy as np
from jax import lax
from jax.experimental import pallas as pl
from jax.experimental.pallas import tpu as pltpu

F32 = jnp.float32
BF16 = jnp.bfloat16

D_MODEL = 1024
MLA_HEADS = 8
QK_NOPE = 64
QK_ROPE = 32
V_HEAD = 64
Q_LORA = 256
KV_LORA = 256
ROPE_THETA = 10000.0
SC_WIDTH = 512
SC_K = 3
CF_WIDTH = 512
CF_K = 31
GM_GROUPS = 4
GM_WIDTH = 512
GM_CHUNK = 128
N_BRANCH = 4
FFN_HIDDEN = 2816
EPS = 1e-6

LANES = 128
HEAD_PAD = LANES
HALO = 16
SUBLANES = 8
CONV_ROWS = 64

N_LATENT = 640
COL_SC_B = N_LATENT
COL_SC_C = COL_SC_B + SC_WIDTH
COL_SC_X = COL_SC_C + SC_WIDTH
COL_CF_A = COL_SC_X + SC_WIDTH
COL_CF_G = COL_CF_A + CF_WIDTH
COL_GM_U = COL_CF_G + CF_WIDTH
COL_GM_V = COL_GM_U + GM_WIDTH
COL_GATE = COL_GM_V + GM_WIDTH
N_IN_PAD = COL_GATE + N_BRANCH * D_MODEL

TM_PREP = 512
QB_ATTN = 256
PAIRS_ATTN = 2
KC_ATTN = 256
TM_MIX = 512
MERGE_COLS = 256
TM_FFN = 512
FFN_CHUNK = 1408
VMEM_LIMIT = 56 * 1024 * 1024


def _rms(x, g):
    ms = jnp.mean(x * x, axis=-1, keepdims=True)
    return x * lax.rsqrt(ms + EPS) * g


def _layer_norm(x, g, b):
    mu = jnp.mean(x, axis=-1, keepdims=True)
    xc = x - mu
    var = jnp.mean(xc * xc, axis=-1, keepdims=True)
    return xc * lax.rsqrt(var + EPS) * g + b


def _sigmoid(x):
    return 0.5 * jnp.tanh(0.5 * x) + 0.5


def _dot(a, b):
    return jnp.dot(a, b, preferred_element_type=F32)


_NT = (((1,), (1,)), ((), ()))
LOG2_E = 1.4426950408889634


def _resident(shape):
    nd = len(shape)
    return pl.BlockSpec(shape, lambda *_: (0,) * nd, pipeline_mode=pl.Buffered(1))


def _layer(shape, l):
    nd = len(shape)
    return pl.BlockSpec((None,) + tuple(shape), lambda *_: (l,) + (0,) * nd,
                        pipeline_mode=pl.Buffered(1))


def _params(*sem):
    return pltpu.CompilerParams(dimension_semantics=sem, vmem_limit_bytes=VMEM_LIMIT)


def _prep_kernel(x_ref, g_ref, wc_ref, qn_ref, kvn_ref, wq_ref, wka_ref, wkr_ref,
                 wkb_ref, wvt_ref, vone_ref, cos_ref, sin_ref, cost_ref, sint_ref,
                 qt_ref, k_ref, vt_ref):
    h = _rms(x_ref[...], g_ref[...]).astype(BF16)
    c = _dot(h, wc_ref[...])
    cq = _rms(c[:, :Q_LORA], qn_ref[...]).astype(BF16)
    ckv = _rms(c[:, Q_LORA:Q_LORA + KV_LORA], kvn_ref[...]).astype(BF16)
    kr = c[:, Q_LORA + KV_LORA:].astype(BF16)
    cos = cos_ref[...]
    sin = sin_ref[...]
    scale = (QK_NOPE + QK_ROPE) ** -0.5 * LOG2_E
    qt = lax.dot_general(wq_ref[...], cq, _NT, preferred_element_type=F32)
    ka = _dot(ckv, wka_ref[...]) + _dot(kr, wkr_ref[...])
    kb = _dot(kr, wkb_ref[...])
    cos_s = cost_ref[...] * scale
    sin_s = sint_ref[...] * scale
    half = QK_ROPE // 2
    for hd in range(MLA_HEADS):
        sl = slice(hd * HEAD_PAD, (hd + 1) * HEAD_PAD)
        qh = qt[sl, :]
        swapped = jnp.concatenate([qh[:QK_NOPE], qh[QK_NOPE + half:QK_NOPE + QK_ROPE],
                                   qh[QK_NOPE:QK_NOPE + half], qh[QK_NOPE + QK_ROPE:]], axis=0)
        qt_ref[sl, :] = (qh * cos_s + swapped * sin_s).astype(BF16)
        k_ref[:, sl] = (ka[:, sl] * cos + kb[:, sl] * sin).astype(BF16)
    vt = lax.dot_general(wvt_ref[...], ckv, _NT, preferred_element_type=F32)
    vt_ref[...] = (vt + vone_ref[...]).astype(BF16)


def _prep(x2, sw, l, rope):
    m = x2.shape[0]
    hp = MLA_HEADS * HEAD_PAD
    row = lambda i: (i, 0)
    col = lambda i: (0, i)
    slab_t = jax.ShapeDtypeStruct((hp, m), BF16)
    return pl.pallas_call(
        _prep_kernel,
        grid=(m // TM_PREP,),
        in_specs=[
            pl.BlockSpec((TM_PREP, D_MODEL), row),
            _layer((1, D_MODEL), l),
            pl.BlockSpec((None, D_MODEL, N_LATENT), lambda i: (l, 0, 0),
                         pipeline_mode=pl.Buffered(1)),
            _layer((1, Q_LORA), l),
            _layer((1, KV_LORA), l),
            _layer((hp, Q_LORA), l),
            _layer((KV_LORA, hp), l),
            _resident((LANES, hp)),
            _resident((LANES, hp)),
            _layer((hp, KV_LORA), l),
            _resident((hp, 1)),
            pl.BlockSpec((TM_PREP, LANES), row),
            pl.BlockSpec((TM_PREP, LANES), row),
            pl.BlockSpec((LANES, TM_PREP), col),
            pl.BlockSpec((LANES, TM_PREP), col),
        ],
        out_specs=[pl.BlockSpec((hp, TM_PREP), col), pl.BlockSpec((TM_PREP, hp), row),
                   pl.BlockSpec((hp, TM_PREP), col)],
        out_shape=[slab_t, jax.ShapeDtypeStruct((m, hp), BF16), slab_t],
        compiler_params=_params("parallel"),
    )(x2, sw["norm_mix_pre"], sw["w_in"], sw["q_norm"], sw["kv_norm"], sw["wq_t"],
      sw["wk_a"], sw["wk_r"], sw["wk_b"], sw["wv_t"], sw["v_one"], *rope)


def _attn_kernel(qt_ref, k_ref, vt_ref, o_ref, *bufs):
    seq = k_ref.shape[0]
    n_kc = seq // KC_ATTN
    n_qb = seq // QB_ATTN
    units = [(pp, blk) for pp in range(PAIRS_ATTN) for blk in range(n_qb)]

    def head_slices(pp):
        return [slice((2 * pp + hh) * HEAD_PAD, (2 * pp + hh + 1) * HEAD_PAD) for hh in range(2)]

    def scores_chunk(unit, s_ref, c, run_max):
        pp, blk = unit
        keys = slice(c * KC_ATTN, (c + 1) * KC_ATTN)
        q_cols = slice(blk * QB_ATTN, (blk + 1) * QB_ATTN)
        for hh, sl in enumerate(head_slices(pp)):
            st = _dot(k_ref[keys, sl], qt_ref[sl, q_cols])
            s_ref[hh, keys, :] = st
            cm = run_max[hh]
            for r in range(KC_ATTN // SUBLANES):
                part = st[r * SUBLANES:(r + 1) * SUBLANES, :]
                cm = part if cm is None else jnp.maximum(cm, part)
            run_max[hh] = cm

    def values_chunk(unit, s_ref, c, row_max, acc):
        keys = slice(c * KC_ATTN, (c + 1) * KC_ATTN)
        for hh, sl in enumerate(head_slices(unit[0])):
            pt = jnp.exp2(s_ref[hh, keys, :] - row_max[hh]).astype(BF16)
            t = _dot(vt_ref[sl, keys], pt)
            acc[hh] = t if acc[hh] is None else acc[hh] + t

    ahead = len(bufs) - 1
    maxes = {}
    for u in range(min(ahead, len(units))):
        maxes[u] = [None, None]
        for c in range(n_kc):
            scores_chunk(units[u], bufs[u % len(bufs)], c, maxes[u])
    for u, (pp, blk) in enumerate(units):
        row_max = [jnp.max(mx, axis=0, keepdims=True) for mx in maxes.pop(u)]
        acc = [None, None]
        nxt = u + ahead
        if nxt < len(units):
            maxes[nxt] = [None, None]
        for c in range(n_kc):
            if nxt < len(units):
                scores_chunk(units[nxt], bufs[nxt % len(bufs)], c, maxes[nxt])
            values_chunk(units[u], bufs[u % len(bufs)], c, row_max, acc)
        halves = [(a * (1.0 / a[V_HEAD:V_HEAD + 1, :]))[:V_HEAD, :] for a in acc]
        o_ref[blk * QB_ATTN:(blk + 1) * QB_ATTN, pp * 2 * V_HEAD:(pp + 1) * 2 * V_HEAD] = (
            jnp.concatenate(halves, axis=0).T.astype(BF16))


def _attention(qt, k, vt, batch, seq):
    m = k.shape[0]
    pair = 2 * PAIRS_ATTN * HEAD_PAD
    return pl.pallas_call(
        _attn_kernel,
        grid=(batch, MLA_HEADS // (2 * PAIRS_ATTN)),
        in_specs=[
            pl.BlockSpec((pair, seq), lambda b, p: (p, b)),
            pl.BlockSpec((seq, pair), lambda b, p: (b, p)),
            pl.BlockSpec((pair, seq), lambda b, p: (p, b)),
        ],
        out_specs=pl.BlockSpec((seq, 2 * PAIRS_ATTN * V_HEAD), lambda b, p: (b, p)),
        out_shape=jax.ShapeDtypeStruct((m, MLA_HEADS * V_HEAD), BF16),
        scratch_shapes=[
            pltpu.VMEM((2, seq, QB_ATTN), F32),
            pltpu.VMEM((2, seq, QB_ATTN), F32),
            pltpu.VMEM((2, seq, QB_ATTN), F32),
        ],
        compiler_params=_params("parallel", "parallel"),
    )(qt, k, vt)


def _depthwise_rows(z_ref, w_ref, bias, out_ref, r0, ls, ntaps):
    nvo = CONV_ROWS // SUBLANES
    shifts = [HALO - ntaps // 2 + k for k in range(ntaps)]
    lo = min(shifts) // SUBLANES
    hi = (max(shifts) + SUBLANES - 1) // SUBLANES + nvo - 1
    win = {i: z_ref[pl.ds(r0 + i * SUBLANES, SUBLANES), ls] for i in range(lo, hi + 1)}
    sub = lax.broadcasted_iota(jnp.int32, (SUBLANES, LANES), 0)
    acc = [bias] * nvo
    for r in range(SUBLANES):
        taps = [k for k in range(ntaps) if shifts[k] % SUBLANES == r]
        if not taps:
            continue
        need = sorted({shifts[k] // SUBLANES + j for k in taps for j in range(nvo)})
        if r == 0:
            sh = {i: win[i] for i in need}
        else:
            rolled = {i: pltpu.roll(win[i], SUBLANES - r, 0)
                      for i in sorted(set(need) | {i + 1 for i in need})}
            sh = {i: jnp.where(sub < SUBLANES - r, rolled[i], rolled[i + 1]) for i in need}
        for k in taps:
            a = shifts[k] // SUBLANES
            wv = w_ref[k, :, ls]
            for j in range(nvo):
                t = sh[a + j] * wv
                acc[j] = t if acc[j] is None else acc[j] + t
    for j in range(nvo):
        out_ref[pl.ds(r0 + j * SUBLANES, SUBLANES), ls] = acc[j]


def _mix_kernel(x_ref, xp_ref, xn_ref, att_ref, gpre_ref, wm_ref, woa_ref, wob_ref, woc_ref,
                wod_ref, wout_ref, scw_ref, cfw_ref, cfb_ref, cflg_ref, cflb_ref, gmlg_ref,
                gmlb_ref, gmws_ref, gmbs_ref, gateb_ref, npost_ref, out_ref,
                h_ref, zc_ref, zs_ref, yb_ref, yc_ref, yd_ref):
    tm = TM_MIX
    i = pl.program_id(1)
    gpre = gpre_ref[...]
    h_ref[0:HALO, :] = _rms(xp_ref[...], gpre).astype(BF16)
    h_ref[HALO:HALO + tm, :] = _rms(x_ref[...], gpre).astype(BF16)
    h_ref[HALO + tm:, :] = _rms(xn_ref[...], gpre).astype(BF16)
    tile = slice(HALO, HALO + tm)

    def proj_ext(col, width):
        return _dot(h_ref[...], wm_ref[:, col:col + width])

    def proj(col, width):
        return _dot(h_ref[tile, :], wm_ref[:, col:col + width])

    row = lax.broadcasted_iota(jnp.int32, (tm + 2 * HALO, 1), 0)
    keep = jnp.logical_and(jnp.logical_or(row >= HALO, i > 0),
                           jnp.logical_or(row < HALO + tm, i < pl.num_programs(1) - 1))
    zc = proj_ext(COL_CF_A, CF_WIDTH) * _sigmoid(proj_ext(COL_CF_G, CF_WIDTH))
    zc_ref[...] = jnp.where(keep, zc, 0.0)
    zs = proj_ext(COL_SC_C, SC_WIDTH) * proj_ext(COL_SC_X, SC_WIDTH)
    zs_ref[...] = jnp.where(keep, zs, 0.0)

    for c in range(tm // CONV_ROWS):
        for lt in range(CF_WIDTH // LANES):
            ls = slice(lt * LANES, (lt + 1) * LANES)
            _depthwise_rows(zc_ref, cfw_ref, cfb_ref[:, ls], yc_ref, c * CONV_ROWS, ls, CF_K)
            _depthwise_rows(zs_ref, scw_ref, None, yb_ref, c * CONV_ROWS, ls, SC_K)
    yc = _layer_norm(yc_ref[...], cflg_ref[...], cflb_ref[...])
    yc = (yc * _sigmoid(yc)).astype(BF16)
    yb = (proj(COL_SC_B, SC_WIDTH) * yb_ref[...]).astype(BF16)

    u = jax.nn.gelu(proj(COL_GM_U, GM_WIDTH))
    v = jax.nn.gelu(proj(COL_GM_V, GM_WIDTH))
    v = _layer_norm(v, gmlg_ref[...], gmlb_ref[...]).astype(BF16)
    gw = GM_WIDTH // GM_GROUPS
    for n in range(tm // GM_CHUNK):
        rs = slice(n * GM_CHUNK, (n + 1) * GM_CHUNK)
        for g in range(GM_GROUPS):
            cs = slice(g * gw, (g + 1) * gw)
            mixed = _dot(gmws_ref[g], v[rs, cs]) + gmbs_ref[:, g:g + 1]
            yd_ref[rs, cs] = (u[rs, cs] * mixed).astype(BF16)

    branches = ((att_ref[...], woa_ref), (yb, wob_ref), (yc, woc_ref), (yd_ref[...], wod_ref))
    mo = None
    for n in range(D_MODEL // MERGE_COLS):
        cs = slice(n * MERGE_COLS, (n + 1) * MERGE_COLS)
        merged = None
        for j, (act, w_ref) in enumerate(branches):
            gl = proj(COL_GATE + j * D_MODEL + n * MERGE_COLS, MERGE_COLS)
            t = _sigmoid(gl + gateb_ref[j:j + 1, cs]) * _dot(act, w_ref[:, cs])
            merged = t if merged is None else merged + t
        t = _dot(merged.astype(BF16), wout_ref[cs, :])
        mo = t if mo is None else mo + t
    out_ref[...] = x_ref[...] + _rms(mo, npost_ref[...])


def _mix(x2, att, sw, l, batch, seq):
    m = x2.shape[0]
    nt = seq // TM_MIX
    hb = TM_MIX // HALO
    last = m // HALO - 1
    tile = lambda b, i: (b * nt + i, 0)
    prev = lambda b, i: (jnp.maximum((b * nt + i) * hb - 1, 0), 0)
    nxt = lambda b, i: (jnp.minimum((b * nt + i + 1) * hb, last), 0)
    return pl.pallas_call(
        _mix_kernel,
        grid=(batch, nt),
        in_specs=[
            pl.BlockSpec((TM_MIX, D_MODEL), tile),
            pl.BlockSpec((HALO, D_MODEL), prev),
            pl.BlockSpec((HALO, D_MODEL), nxt),
            pl.BlockSpec((TM_MIX, MLA_HEADS * V_HEAD), tile),
            _layer((1, D_MODEL), l),
            _layer((D_MODEL, N_IN_PAD), l),
            _layer((MLA_HEADS * V_HEAD, D_MODEL), l),
            _layer((SC_WIDTH, D_MODEL), l),
            _layer((CF_WIDTH, D_MODEL), l),
            _layer((GM_WIDTH, D_MODEL), l),
            _layer((D_MODEL, D_MODEL), l),
            _layer((SC_K, SUBLANES, SC_WIDTH), l),
            _layer((CF_K, SUBLANES, CF_WIDTH), l),
            _layer((SUBLANES, CF_WIDTH), l),
            _layer((1, CF_WIDTH), l),
            _layer((1, CF_WIDTH), l),
            _layer((1, GM_WIDTH), l),
            _layer((1, GM_WIDTH), l),
            _layer((GM_GROUPS, GM_CHUNK, GM_CHUNK), l),
            _layer((GM_CHUNK, GM_GROUPS), l),
            _layer((N_BRANCH, D_MODEL), l),
            _layer((1, D_MODEL), l),
        ],
        out_specs=pl.BlockSpec((TM_MIX, D_MODEL), tile),
        out_shape=jax.ShapeDtypeStruct((m, D_MODEL), F32),
        scratch_shapes=[
            pltpu.VMEM((TM_MIX + 2 * HALO, D_MODEL), BF16),
            pltpu.VMEM((TM_MIX + 2 * HALO, CF_WIDTH), F32),
            pltpu.VMEM((TM_MIX + 2 * HALO, SC_WIDTH), F32),
            pltpu.VMEM((TM_MIX, SC_WIDTH), F32),
            pltpu.VMEM((TM_MIX, CF_WIDTH), F32),
            pltpu.VMEM((TM_MIX, GM_WIDTH), BF16),
        ],
        compiler_params=_params("parallel", "arbitrary"),
    )(x2, x2, x2, att, sw["norm_mix_pre"], sw["w_in"], sw["w_o_mla"], sw["w_o_sc"], sw["w_o_cf"],
      sw["w_o_gm"], sw["w_out"], sw["sc_conv_w"], sw["cf_conv_w"], sw["cf_conv_b"], sw["cf_ln_g"],
      sw["cf_ln_b"], sw["gm_ln_g"], sw["gm_ln_b"], sw["gm_ws"], sw["gm_bs_t"], sw["gate_b"],
      sw["norm_mix_post"])


def _ffn_kernel(x_ref, gpre_ref, wi_ref, wo_ref, gpost_ref, out_ref):
    x = x_ref[...]
    h = _rms(x, gpre_ref[...]).astype(BF16)
    acc = None
    for c in range(FFN_HIDDEN // FFN_CHUNK):
        cs = slice(c * FFN_CHUNK, (c + 1) * FFN_CHUNK)
        fg = _dot(h, wi_ref[:, cs])
        fu = _dot(h, wi_ref[:, FFN_HIDDEN + c * FFN_CHUNK:FFN_HIDDEN + (c + 1) * FFN_CHUNK])
        f = (fg * _sigmoid(fg) * fu).astype(BF16)
        t = _dot(f, wo_ref[cs, :])
        acc = t if acc is None else acc + t
    out_ref[...] = x + _rms(acc, gpost_ref[...])


def _ffn(x2, sw, l):
    m = x2.shape[0]
    row = lambda i: (i, 0)
    return pl.pallas_call(
        _ffn_kernel,
        grid=(m // TM_FFN,),
        in_specs=[
            pl.BlockSpec((TM_FFN, D_MODEL), row),
            _layer((1, D_MODEL), l),
            _layer((D_MODEL, 2 * FFN_HIDDEN), l),
            _layer((FFN_HIDDEN, D_MODEL), l),
            _layer((1, D_MODEL), l),
        ],
        out_specs=pl.BlockSpec((TM_FFN, D_MODEL), row),
        out_shape=jax.ShapeDtypeStruct((m, D_MODEL), F32),
        compiler_params=_params("parallel"),
    )(x2, sw["norm_ffn_pre"], sw["w_ffn_in"], sw["w_ffn_out"], sw["norm_ffn_post"])


def _head_slab(w, src_cols, dst_off, width):
    depth, rows = w.shape[:2]
    out = jnp.zeros((depth, rows, MLA_HEADS, HEAD_PAD), F32)
    src = w.reshape(depth, rows, MLA_HEADS, -1)[..., src_cols:src_cols + width]
    return out.at[..., dst_off:dst_off + width].set(src)


def _rope_place():
    half = QK_ROPE // 2
    plain = np.zeros((LANES, MLA_HEADS, HEAD_PAD), np.float32)
    rot = np.zeros((LANES, MLA_HEADS, HEAD_PAD), np.float32)
    for j in range(QK_ROPE):
        plain[j, :, QK_NOPE + j] = 1.0
    for j in range(half):
        rot[half + j, :, QK_NOPE + j] = -1.0
        rot[j, :, QK_NOPE + half + j] = 1.0
    shape = (LANES, MLA_HEADS * HEAD_PAD)
    return jnp.asarray(plain.reshape(shape), BF16), jnp.asarray(rot.reshape(shape), BF16)


def _stacked_weights(p):
    hp = MLA_HEADS * HEAD_PAD
    w_in = p["w_in"]
    depth = w_in.shape[0]
    o = Q_LORA + KV_LORA + QK_ROPE
    w_in_pad = jnp.zeros((depth, D_MODEL, N_IN_PAD), BF16)
    w_in_pad = lax.dynamic_update_slice(w_in_pad, w_in[:, :, :o].astype(BF16), (0, 0, 0))
    w_in_pad = lax.dynamic_update_slice(w_in_pad, w_in[:, :, o:].astype(BF16), (0, 0, N_LATENT))

    w_uq = p["w_uq"]
    wq = _head_slab(w_uq, 0, 0, QK_NOPE + QK_ROPE)
    w_ukv = p["w_ukv"]
    wk_a = _head_slab(w_ukv, 0, 0, QK_NOPE)
    wv = _head_slab(w_ukv, QK_NOPE, 0, V_HEAD)
    wk_r, wk_b = _rope_place()
    v_one = np.zeros((MLA_HEADS, HEAD_PAD, 1), np.float32)
    v_one[:, V_HEAD] = 1.0

    def row(a):
        return a[:, None, :]

    def rows8(a):
        return jnp.broadcast_to(a[..., None, :], a.shape[:-1] + (SUBLANES, a.shape[-1]))

    return {
        "norm_mix_pre": row(p["norm_mix_pre"]),
        "w_in": w_in_pad,
        "q_norm": row(p["mla_q_norm"]),
        "kv_norm": row(p["mla_kv_norm"]),
        "wq_t": jnp.swapaxes(wq.reshape(depth, Q_LORA, hp), 1, 2).astype(BF16),
        "wk_a": wk_a.reshape(depth, KV_LORA, hp).astype(BF16),
        "wk_r": wk_r,
        "wk_b": wk_b,
        "wv_t": jnp.swapaxes(wv.reshape(depth, KV_LORA, hp), 1, 2).astype(BF16),
        "v_one": jnp.asarray(v_one.reshape(hp, 1)),
        "w_o_mla": p["w_o_mla"].astype(BF16),
        "w_o_sc": p["w_o_sc"].astype(BF16),
        "w_o_cf": p["w_o_cf"].astype(BF16),
        "w_o_gm": p["w_o_gm"].astype(BF16),
        "w_out": p["w_out"].astype(BF16),
        "sc_conv_w": rows8(p["sc_conv_w"]),
        "cf_conv_w": rows8(p["cf_conv_w"]),
        "cf_conv_b": rows8(p["cf_conv_b"]),
        "cf_ln_g": row(p["cf_ln_g"]),
        "cf_ln_b": row(p["cf_ln_b"]),
        "gm_ln_g": row(p["gm_ln_g"]),
        "gm_ln_b": row(p["gm_ln_b"]),
        "gm_ws": p["gm_ws"].astype(BF16),
        "gm_bs_t": jnp.swapaxes(p["gm_bs"], 1, 2),
        "gate_b": p["gate_b"],
        "norm_mix_post": row(p["norm_mix_post"]),
        "norm_ffn_pre": row(p["norm_ffn_pre"]),
        "w_ffn_in": p["w_ffn_in"].astype(BF16),
        "w_ffn_out": p["w_ffn_out"].astype(BF16),
        "norm_ffn_post": row(p["norm_ffn_post"]),
    }


def _rope_tables(positions):
    m = positions.size
    inv_freq = ROPE_THETA ** (-jnp.arange(0, QK_ROPE, 2, dtype=F32) / QK_ROPE)
    ang = inv_freq[:, None] * positions.astype(F32).reshape(1, m)
    cos, sin = jnp.cos(ang), jnp.sin(ang)
    ones = jnp.ones((QK_NOPE, m), F32)
    zeros = jnp.zeros((QK_NOPE, m), F32)
    pad = jnp.zeros((HEAD_PAD - QK_NOPE - QK_ROPE, m), F32)
    cos_t = jnp.concatenate([ones, cos, cos, pad], axis=0)
    sin_t = jnp.concatenate([zeros, sin, sin, pad], axis=0)
    sin_q = jnp.concatenate([zeros, -sin, sin, pad], axis=0)
    return cos_t.T, sin_t.T, cos_t, sin_q


def kernel(x, positions, norm_mix_pre, w_in, mla_q_norm, w_uq, mla_kv_norm, w_ukv, w_o_mla, sc_conv_w, w_o_sc, cf_conv_w, cf_conv_b, cf_ln_g, cf_ln_b, w_o_cf, gm_ln_g, gm_ln_b, gm_ws, gm_bs, w_o_gm, gate_b, w_out, norm_mix_post, norm_ffn_pre, w_ffn_in, w_ffn_out, norm_ffn_post):
    p = dict(norm_mix_pre=norm_mix_pre, w_in=w_in, mla_q_norm=mla_q_norm, w_uq=w_uq,
             mla_kv_norm=mla_kv_norm, w_ukv=w_ukv, w_o_mla=w_o_mla, sc_conv_w=sc_conv_w,
             w_o_sc=w_o_sc, cf_conv_w=cf_conv_w, cf_conv_b=cf_conv_b, cf_ln_g=cf_ln_g,
             cf_ln_b=cf_ln_b, w_o_cf=w_o_cf, gm_ln_g=gm_ln_g, gm_ln_b=gm_ln_b, gm_ws=gm_ws,
             gm_bs=gm_bs, w_o_gm=w_o_gm, gate_b=gate_b, w_out=w_out, norm_mix_post=norm_mix_post,
             norm_ffn_pre=norm_ffn_pre, w_ffn_in=w_ffn_in, w_ffn_out=w_ffn_out,
             norm_ffn_post=norm_ffn_post)
    batch, seq, d = x.shape
    depth = w_in.shape[0]
    m = batch * seq
    assert d == D_MODEL and m % TM_PREP == 0 and m % TM_FFN == 0
    assert seq % QB_ATTN == 0 and seq % KC_ATTN == 0 and seq % TM_MIX == 0
    rope = _rope_tables(positions)
    x2 = x.reshape(m, d)
    sw = _stacked_weights(p)
    for l in range(depth):
        qt, k, vt = _prep(x2, sw, l, rope)
        att = _attention(qt, k, vt, batch, seq)
        x2 = _mix(x2, att, sw, l, batch, seq)
        x2 = _ffn(x2, sw, l)
    return x2.reshape(batch, seq, d)
```

```python
import jax
import jax.numpy as jnp
import numpy as np
from jax import lax
from jax.experimental import pallas as pl
from jax.experimental.pallas import tpu as pltpu

F32 = jnp.float32
BF16 = jnp.bfloat16

D_MODEL = 1024
MLA_HEADS = 8
QK_NOPE = 64
QK_ROPE = 32
V_HEAD = 64
Q_LORA = 256
KV_LORA = 256
ROPE_THETA = 10000.0
SC_WIDTH = 512
SC_K = 3
CF_WIDTH = 512
CF_K = 31
GM_GROUPS = 4
GM_WIDTH = 512
GM_CHUNK = 128
N_BRANCH = 4
FFN_HIDDEN = 2816
EPS = 1e-6

LANES = 128
HEAD_PAD = LANES
HALO = 16
SUBLANES = 8
CONV_ROWS = 64

N_LATENT = 640
COL_SC_B = N_LATENT
COL_SC_C = COL_SC_B + SC_WIDTH
COL_SC_X = COL_SC_C + SC_WIDTH
COL_CF_A = COL_SC_X + SC_WIDTH
COL_CF_G = COL_CF_A + CF_WIDTH
COL_GM_U = COL_CF_G + CF_WIDTH
COL_GM_V = COL_GM_U + GM_WIDTH
COL_GATE = COL_GM_V + GM_WIDTH
N_IN_PAD = COL_GATE + N_BRANCH * D_MODEL

TM_PREP = 512
QB_ATTN = 256
PAIRS_ATTN = 2
KC_ATTN = 256
TM_MIX = 512
MERGE_COLS = 256
TM_FFN = 512
FFN_CHUNK = 1408
VMEM_LIMIT = 56 * 1024 * 1024


def _rms(x, g):
    ms = jnp.mean(x * x, axis=-1, keepdims=True)
    return x * lax.rsqrt(ms + EPS) * g


def _layer_norm(x, g, b):
    mu = jnp.mean(x, axis=-1, keepdims=True)
    xc = x - mu
    var = jnp.mean(xc * xc, axis=-1, keepdims=True)
    return xc * lax.rsqrt(var + EPS) * g + b


def _sigmoid(x):
    return 0.5 * jnp.tanh(0.5 * x) + 0.5


def _dot(a, b):
    return jnp.dot(a, b, preferred_element_type=F32)


_NT = (((1,), (1,)), ((), ()))
LOG2_E = 1.4426950408889634


def _resident(shape):
    nd = len(shape)
    return pl.BlockSpec(shape, lambda *_: (0,) * nd, pipeline_mode=pl.Buffered(1))


def _layer(shape, l):
    nd = len(shape)
    return pl.BlockSpec((None,) + tuple(shape), lambda *_: (l,) + (0,) * nd,
                        pipeline_mode=pl.Buffered(1))


def _params(*sem):
    return pltpu.CompilerParams(dimension_semantics=sem, vmem_limit_bytes=VMEM_LIMIT)


def _prep_kernel(x_ref, g_ref, wc_ref, qn_ref, kvn_ref, wq_ref, wka_ref, wkr_ref,
                 wkb_ref, wvt_ref, vone_ref, cos_ref, sin_ref, cost_ref, sint_ref,
                 qt_ref, k_ref, vt_ref):
    h = _rms(x_ref[...], g_ref[...]).astype(BF16)
    c = _dot(h, wc_ref[...])
    cq = _rms(c[:, :Q_LORA], qn_ref[...]).astype(BF16)
    ckv = _rms(c[:, Q_LORA:Q_LORA + KV_LORA], kvn_ref[...]).astype(BF16)
    kr = c[:, Q_LORA + KV_LORA:].astype(BF16)
    cos = cos_ref[...]
    sin = sin_ref[...]
    scale = (QK_NOPE + QK_ROPE) ** -0.5 * LOG2_E
    qt = lax.dot_general(wq_ref[...], cq, _NT, preferred_element_type=F32)
    ka = _dot(ckv, wka_ref[...]) + _dot(kr, wkr_ref[...])
    kb = _dot(kr, wkb_ref[...])
    cos_s = cost_ref[...] * scale
    sin_s = sint_ref[...] * scale
    half = QK_ROPE // 2
    for hd in range(MLA_HEADS):
        sl = slice(hd * HEAD_PAD, (hd + 1) * HEAD_PAD)
        qh = qt[sl, :]
        swapped = jnp.concatenate([qh[:QK_NOPE], qh[QK_NOPE + half:QK_NOPE + QK_ROPE],
                                   qh[QK_NOPE:QK_NOPE + half], qh[QK_NOPE + QK_ROPE:]], axis=0)
        qt_ref[sl, :] = (qh * cos_s + swapped * sin_s).astype(BF16)
        k_ref[:, sl] = (ka[:, sl] * cos + kb[:, sl] * sin).astype(BF16)
    vt = lax.dot_general(wvt_ref[...], ckv, _NT, preferred_element_type=F32)
    vt_ref[...] = (vt + vone_ref[...]).astype(BF16)


def _prep(x2, sw, l, rope):
    m = x2.shape[0]
    hp = MLA_HEADS * HEAD_PAD
    row = lambda i: (i, 0)
    col = lambda i: (0, i)
    slab_t = jax.ShapeDtypeStruct((hp, m), BF16)
    return pl.pallas_call(
        _prep_kernel,
        grid=(m // TM_PREP,),
        in_specs=[
            pl.BlockSpec((TM_PREP, D_MODEL), row),
            _layer((1, D_MODEL), l),
            pl.BlockSpec((None, D_MODEL, N_LATENT), lambda i: (l, 0, 0),
                         pipeline_mode=pl.Buffered(1)),
            _layer((1, Q_LORA), l),
            _layer((1, KV_LORA), l),
            _layer((hp, Q_LORA), l),
            _layer((KV_LORA, hp), l),
            _resident((LANES, hp)),
            _resident((LANES, hp)),
            _layer((hp, KV_LORA), l),
            _resident((hp, 1)),
            pl.BlockSpec((TM_PREP, LANES), row),
            pl.BlockSpec((TM_PREP, LANES), row),
            pl.BlockSpec((LANES, TM_PREP), col),
            pl.BlockSpec((LANES, TM_PREP), col),
        ],
        out_specs=[pl.BlockSpec((hp, TM_PREP), col), pl.BlockSpec((TM_PREP, hp), row),
                   pl.BlockSpec((hp, TM_PREP), col)],
        out_shape=[slab_t, jax.ShapeDtypeStruct((m, hp), BF16), slab_t],
        compiler_params=_params("parallel"),
    )(x2, sw["norm_mix_pre"], sw["w_in"], sw["q_norm"], sw["kv_norm"], sw["wq_t"],
      sw["wk_a"], sw["wk_r"], sw["wk_b"], sw["wv_t"], sw["v_one"], *rope)


def _attn_kernel(zero_ref, qt_ref, k_ref, vt_ref, o_ref, *bufs):
    seq = k_ref.shape[0]
    n_kc = seq // KC_ATTN
    n_qb = seq // QB_ATTN
    units = [(pp, blk) for pp in range(PAIRS_ATTN) for blk in range(n_qb)]

    def head_slices(pp):
        return [slice((2 * pp + hh) * HEAD_PAD, (2 * pp + hh + 1) * HEAD_PAD) for hh in range(2)]

    def scores_chunk(unit, s_ref, c, run_max):
        pp, blk = unit
        keys = slice(c * KC_ATTN, (c + 1) * KC_ATTN)
        q_cols = slice(blk * QB_ATTN, (blk + 1) * QB_ATTN)
        for hh, sl in enumerate(head_slices(pp)):
            st = _dot(k_ref[keys, sl], qt_ref[sl, q_cols])
            s_ref[hh, keys, :] = st
            cm = run_max[hh]
            for r in range(KC_ATTN // SUBLANES):
                part = st[r * SUBLANES:(r + 1) * SUBLANES, :]
                cm = part if cm is None else jnp.maximum(cm, part)
            run_max[hh] = cm

    base = pl.multiple_of(zero_ref[0], KC_ATTN)

    def values_chunk(unit, s_ref, c, row_max, acc):
        keys = slice(c * KC_ATTN, (c + 1) * KC_ATTN)
        for hh, sl in enumerate(head_slices(unit[0])):
            st = s_ref[hh, pl.ds(base + c * KC_ATTN, KC_ATTN), :]
            pt = jnp.exp2(st - row_max[hh]).astype(BF16)
            t = _dot(vt_ref[sl, keys], pt)
            acc[hh] = t if acc[hh] is None else acc[hh] + t

    ahead = len(bufs) - 1
    maxes = {}
    for u in range(min(ahead, len(units))):
        maxes[u] = [None, None]
        for c in range(n_kc):
            scores_chunk(units[u], bufs[u % len(bufs)], c, maxes[u])
    for u, (pp, blk) in enumerate(units):
        row_max = [jnp.max(mx, axis=0, keepdims=True) for mx in maxes.pop(u)]
        acc = [None, None]
        nxt = u + ahead
        if nxt < len(units):
            maxes[nxt] = [None, None]
        for c in range(n_kc):
            if nxt < len(units):
                scores_chunk(units[nxt], bufs[nxt % len(bufs)], c, maxes[nxt])
            values_chunk(units[u], bufs[u % len(bufs)], c, row_max, acc)
        halves = [(a * (1.0 / a[V_HEAD:V_HEAD + 1, :]))[:V_HEAD, :] for a in acc]
        o_ref[blk * QB_ATTN:(blk + 1) * QB_ATTN, pp * 2 * V_HEAD:(pp + 1) * 2 * V_HEAD] = (
            jnp.concatenate(halves, axis=0).T.astype(BF16))


def _attention(qt, k, vt, batch, seq):
    m = k.shape[0]
    pair = 2 * PAIRS_ATTN * HEAD_PAD
    return pl.pallas_call(
        _attn_kernel,
        grid=(batch, MLA_HEADS // (2 * PAIRS_ATTN)),
        in_specs=[
            pl.BlockSpec(memory_space=pltpu.SMEM),
            pl.BlockSpec((pair, seq), lambda b, p: (p, b)),
            pl.BlockSpec((seq, pair), lambda b, p: (b, p)),
            pl.BlockSpec((pair, seq), lambda b, p: (p, b)),
        ],
        out_specs=pl.BlockSpec((seq, 2 * PAIRS_ATTN * V_HEAD), lambda b, p: (b, p)),
        out_shape=jax.ShapeDtypeStruct((m, MLA_HEADS * V_HEAD), BF16),
        scratch_shapes=[
            pltpu.VMEM((2, seq, QB_ATTN), F32),
            pltpu.VMEM((2, seq, QB_ATTN), F32),
            pltpu.VMEM((2, seq, QB_ATTN), F32),
        ],
        compiler_params=_params("parallel", "parallel"),
    )(jnp.zeros((1,), jnp.int32), qt, k, vt)


def _depthwise_rows(z_ref, w_ref, bias, out_ref, r0, ls, ntaps):
    nvo = CONV_ROWS // SUBLANES
    shifts = [HALO - ntaps // 2 + k for k in range(ntaps)]
    lo = min(shifts) // SUBLANES
    hi = (max(shifts) + SUBLANES - 1) // SUBLANES + nvo - 1
    win = {i: z_ref[pl.ds(r0 + i * SUBLANES, SUBLANES), ls] for i in range(lo, hi + 1)}
    sub = lax.broadcasted_iota(jnp.int32, (SUBLANES, LANES), 0)
    acc = [bias] * nvo
    for r in range(SUBLANES):
        taps = [k for k in range(ntaps) if shifts[k] % SUBLANES == r]
        if not taps:
            continue
        need = sorted({shifts[k] // SUBLANES + j for k in taps for j in range(nvo)})
        if r == 0:
            sh = {i: win[i] for i in need}
        else:
            rolled = {i: pltpu.roll(win[i], SUBLANES - r, 0)
                      for i in sorted(set(need) | {i + 1 for i in need})}
            sh = {i: jnp.where(sub < SUBLANES - r, rolled[i], rolled[i + 1]) for i in need}
        for k in taps:
            a = shifts[k] // SUBLANES
            wv = w_ref[k, :, ls]
            for j in range(nvo):
                t = sh[a + j] * wv
                acc[j] = t if acc[j] is None else acc[j] + t
    for j in range(nvo):
        out_ref[pl.ds(r0 + j * SUBLANES, SUBLANES), ls] = acc[j]


def _mix_kernel(x_ref, xp_ref, xn_ref, att_ref, gpre_ref, wm_ref, woa_ref, wob_ref, woc_ref,
                wod_ref, wout_ref, scw_ref, cfw_ref, cfb_ref, cflg_ref, cflb_ref, gmlg_ref,
                gmlb_ref, gmws_ref, gmbs_ref, gateb_ref, npost_ref, out_ref,
                h_ref, zc_ref, zs_ref, yb_ref, yc_ref, yd_ref):
    tm = TM_MIX
    i = pl.program_id(1)
    gpre = gpre_ref[...]
    h_ref[0:HALO, :] = _rms(xp_ref[...], gpre).astype(BF16)
    h_ref[HALO:HALO + tm, :] = _rms(x_ref[...], gpre).astype(BF16)
    h_ref[HALO + tm:, :] = _rms(xn_ref[...], gpre).astype(BF16)
    tile = slice(HALO, HALO + tm)

    def proj_ext(col, width):
        return _dot(h_ref[...], wm_ref[:, col:col + width])

    def proj(col, width):
        return _dot(h_ref[tile, :], wm_ref[:, col:col + width])

    row = lax.broadcasted_iota(jnp.int32, (tm + 2 * HALO, 1), 0)
    keep = jnp.logical_and(jnp.logical_or(row >= HALO, i > 0),
                           jnp.logical_or(row < HALO + tm, i < pl.num_programs(1) - 1))
    zc = proj_ext(COL_CF_A, CF_WIDTH) * _sigmoid(proj_ext(COL_CF_G, CF_WIDTH))
    zc_ref[...] = jnp.where(keep, zc, 0.0)
    zs = proj_ext(COL_SC_C, SC_WIDTH) * proj_ext(COL_SC_X, SC_WIDTH)
    zs_ref[...] = jnp.where(keep, zs, 0.0)

    for c in range(tm // CONV_ROWS):
        for lt in range(CF_WIDTH // LANES):
            ls = slice(lt * LANES, (lt + 1) * LANES)
            _depthwise_rows(zc_ref, cfw_ref, cfb_ref[:, ls], yc_ref, c * CONV_ROWS, ls, CF_K)
            _depthwise_rows(zs_ref, scw_ref, None, yb_ref, c * CONV_ROWS, ls, SC_K)
    yc = _layer_norm(yc_ref[...], cflg_ref[...], cflb_ref[...])
    yc = (yc * _sigmoid(yc)).astype(BF16)
    yb = (proj(COL_SC_B, SC_WIDTH) * yb_ref[...]).astype(BF16)

    u = jax.nn.gelu(proj(COL_GM_U, GM_WIDTH))
    v = jax.nn.gelu(proj(COL_GM_V, GM_WIDTH))
    v = _layer_norm(v, gmlg_ref[...], gmlb_ref[...]).astype(BF16)
    gw = GM_WIDTH // GM_GROUPS
    for n in range(tm // GM_CHUNK):
        rs = slice(n * GM_CHUNK, (n + 1) * GM_CHUNK)
        for g in range(GM_GROUPS):
            cs = slice(g * gw, (g + 1) * gw)
            mixed = _dot(gmws_ref[g], v[rs, cs]) + gmbs_ref[:, g:g + 1]
            yd_ref[rs, cs] = (u[rs, cs] * mixed).astype(BF16)

    branches = ((att_ref[...], woa_ref), (yb, wob_ref), (yc, woc_ref), (yd_ref[...], wod_ref))
    mo = None
    for n in range(D_MODEL // MERGE_COLS):
        cs = slice(n * MERGE_COLS, (n + 1) * MERGE_COLS)
        merged = None
        for j, (act, w_ref) in enumerate(branches):
            gl = proj(COL_GATE + j * D_MODEL + n * MERGE_COLS, MERGE_COLS)
            t = _sigmoid(gl + gateb_ref[j:j + 1, cs]) * _dot(act, w_ref[:, cs])
            merged = t if merged is None else merged + t
        t = _dot(merged.astype(BF16), wout_ref[cs, :])
        mo = t if mo is None else mo + t
    out_ref[...] = x_ref[...] + _rms(mo, npost_ref[...])


def _mix(x2, att, sw, l, batch, seq):
    m = x2.shape[0]
    nt = seq // TM_MIX
    hb = TM_MIX // HALO
    last = m // HALO - 1
    tile = lambda b, i: (b * nt + i, 0)
    prev = lambda b, i: (jnp.maximum((b * nt + i) * hb - 1, 0), 0)
    nxt = lambda b, i: (jnp.minimum((b * nt + i + 1) * hb, last), 0)
    return pl.pallas_call(
        _mix_kernel,
        grid=(batch, nt),
        in_specs=[
            pl.BlockSpec((TM_MIX, D_MODEL), tile),
            pl.BlockSpec((HALO, D_MODEL), prev),
            pl.BlockSpec((HALO, D_MODEL), nxt),
            pl.BlockSpec((TM_MIX, MLA_HEADS * V_HEAD), tile),
            _layer((1, D_MODEL), l),
            _layer((D_MODEL, N_IN_PAD), l),
            _layer((MLA_HEADS * V_HEAD, D_MODEL), l),
            _layer((SC_WIDTH, D_MODEL), l),
            _layer((CF_WIDTH, D_MODEL), l),
            _layer((GM_WIDTH, D_MODEL), l),
            _layer((D_MODEL, D_MODEL), l),
            _layer((SC_K, SUBLANES, SC_WIDTH), l),
            _layer((CF_K, SUBLANES, CF_WIDTH), l),
            _layer((SUBLANES, CF_WIDTH), l),
            _layer((1, CF_WIDTH), l),
            _layer((1, CF_WIDTH), l),
            _layer((1, GM_WIDTH), l),
            _layer((1, GM_WIDTH), l),
            _layer((GM_GROUPS, GM_CHUNK, GM_CHUNK), l),
            _layer((GM_CHUNK, GM_GROUPS), l),
            _layer((N_BRANCH, D_MODEL), l),
            _layer((1, D_MODEL), l),
        ],
        out_specs=pl.BlockSpec((TM_MIX, D_MODEL), tile),
        out_shape=jax.ShapeDtypeStruct((m, D_MODEL), F32),
        scratch_shapes=[
            pltpu.VMEM((TM_MIX + 2 * HALO, D_MODEL), BF16),
            pltpu.VMEM((TM_MIX + 2 * HALO, CF_WIDTH), F32),
            pltpu.VMEM((TM_MIX + 2 * HALO, SC_WIDTH), F32),
            pltpu.VMEM((TM_MIX, SC_WIDTH), F32),
            pltpu.VMEM((TM_MIX, CF_WIDTH), F32),
            pltpu.VMEM((TM_MIX, GM_WIDTH), BF16),
        ],
        compiler_params=_params("parallel", "arbitrary"),
    )(x2, x2, x2, att, sw["norm_mix_pre"], sw["w_in"], sw["w_o_mla"], sw["w_o_sc"], sw["w_o_cf"],
      sw["w_o_gm"], sw["w_out"], sw["sc_conv_w"], sw["cf_conv_w"], sw["cf_conv_b"], sw["cf_ln_g"],
      sw["cf_ln_b"], sw["gm_ln_g"], sw["gm_ln_b"], sw["gm_ws"], sw["gm_bs_t"], sw["gate_b"],
      sw["norm_mix_post"])


def _ffn_kernel(x_ref, gpre_ref, wi_ref, wo_ref, gpost_ref, out_ref):
    x = x_ref[...]
    h = _rms(x, gpre_ref[...]).astype(BF16)
    acc = None
    for c in range(FFN_HIDDEN // FFN_CHUNK):
        cs = slice(c * FFN_CHUNK, (c + 1) * FFN_CHUNK)
        fg = _dot(h, wi_ref[:, cs])
        fu = _dot(h, wi_ref[:, FFN_HIDDEN + c * FFN_CHUNK:FFN_HIDDEN + (c + 1) * FFN_CHUNK])
        f = (fg * _sigmoid(fg) * fu).astype(BF16)
        t = _dot(f, wo_ref[cs, :])
        acc = t if acc is None else acc + t
    out_ref[...] = x + _rms(acc, gpost_ref[...])


def _ffn(x2, sw, l):
    m = x2.shape[0]
    row = lambda i: (i, 0)
    return pl.pallas_call(
        _ffn_kernel,
        grid=(m // TM_FFN,),
        in_specs=[
            pl.BlockSpec((TM_FFN, D_MODEL), row),
            _layer((1, D_MODEL), l),
            _layer((D_MODEL, 2 * FFN_HIDDEN), l),
            _layer((FFN_HIDDEN, D_MODEL), l),
            _layer((1, D_MODEL), l),
        ],
        out_specs=pl.BlockSpec((TM_FFN, D_MODEL), row),
        out_shape=jax.ShapeDtypeStruct((m, D_MODEL), F32),
        compiler_params=_params("parallel"),
    )(x2, sw["norm_ffn_pre"], sw["w_ffn_in"], sw["w_ffn_out"], sw["norm_ffn_post"])


def _head_slab(w, src_cols, dst_off, width):
    depth, rows = w.shape[:2]
    out = jnp.zeros((depth, rows, MLA_HEADS, HEAD_PAD), F32)
    src = w.reshape(depth, rows, MLA_HEADS, -1)[..., src_cols:src_cols + width]
    return out.at[..., dst_off:dst_off + width].set(src)


def _rope_place():
    half = QK_ROPE // 2
    plain = np.zeros((LANES, MLA_HEADS, HEAD_PAD), np.float32)
    rot = np.zeros((LANES, MLA_HEADS, HEAD_PAD), np.float32)
    for j in range(QK_ROPE):
        plain[j, :, QK_NOPE + j] = 1.0
    for j in range(half):
        rot[half + j, :, QK_NOPE + j] = -1.0
        rot[j, :, QK_NOPE + half + j] = 1.0
    shape = (LANES, MLA_HEADS * HEAD_PAD)
    return jnp.asarray(plain.reshape(shape), BF16), jnp.asarray(rot.reshape(shape), BF16)


def _stacked_weights(p):
    hp = MLA_HEADS * HEAD_PAD
    w_in = p["w_in"]
    depth = w_in.shape[0]
    o = Q_LORA + KV_LORA + QK_ROPE
    w_in_pad = jnp.zeros((depth, D_MODEL, N_IN_PAD), BF16)
    w_in_pad = lax.dynamic_update_slice(w_in_pad, w_in[:, :, :o].astype(BF16), (0, 0, 0))
    w_in_pad = lax.dynamic_update_slice(w_in_pad, w_in[:, :, o:].astype(BF16), (0, 0, N_LATENT))

    w_uq = p["w_uq"]
    wq = _head_slab(w_uq, 0, 0, QK_NOPE + QK_ROPE)
    w_ukv = p["w_ukv"]
    wk_a = _head_slab(w_ukv, 0, 0, QK_NOPE)
    wv = _head_slab(w_ukv, QK_NOPE, 0, V_HEAD)
    wk_r, wk_b = _rope_place()
    v_one = np.zeros((MLA_HEADS, HEAD_PAD, 1), np.float32)
    v_one[:, V_HEAD] = 1.0

    def row(a):
        return a[:, None, :]

    def rows8(a):
        return jnp.broadcast_to(a[..., None, :], a.shape[:-1] + (SUBLANES, a.shape[-1]))

    return {
        "norm_mix_pre": row(p["norm_mix_pre"]),
        "w_in": w_in_pad,
        "q_norm": row(p["mla_q_norm"]),
        "kv_norm": row(p["mla_kv_norm"]),
        "wq_t": jnp.swapaxes(wq.reshape(depth, Q_LORA, hp), 1, 2).astype(BF16),
        "wk_a": wk_a.reshape(depth, KV_LORA, hp).astype(BF16),
        "wk_r": wk_r,
        "wk_b": wk_b,
        "wv_t": jnp.swapaxes(wv.reshape(depth, KV_LORA, hp), 1, 2).astype(BF16),
        "v_one": jnp.asarray(v_one.reshape(hp, 1)),
        "w_o_mla": p["w_o_mla"].astype(BF16),
        "w_o_sc": p["w_o_sc"].astype(BF16),
        "w_o_cf": p["w_o_cf"].astype(BF16),
        "w_o_gm": p["w_o_gm"].astype(BF16),
        "w_out": p["w_out"].astype(BF16),
        "sc_conv_w": rows8(p["sc_conv_w"]),
        "cf_conv_w": rows8(p["cf_conv_w"]),
        "cf_conv_b": rows8(p["cf_conv_b"]),
        "cf_ln_g": row(p["cf_ln_g"]),
        "cf_ln_b": row(p["cf_ln_b"]),
        "gm_ln_g": row(p["gm_ln_g"]),
        "gm_ln_b": row(p["gm_ln_b"]),
        "gm_ws": p["gm_ws"].astype(BF16),
        "gm_bs_t": jnp.swapaxes(p["gm_bs"], 1, 2),
        "gate_b": p["gate_b"],
        "norm_mix_post": row(p["norm_mix_post"]),
        "norm_ffn_pre": row(p["norm_ffn_pre"]),
        "w_ffn_in": p["w_ffn_in"].astype(BF16),
        "w_ffn_out": p["w_ffn_out"].astype(BF16),
        "norm_ffn_post": row(p["norm_ffn_post"]),
    }


def _rope_tables(positions):
    m = positions.size
    inv_freq = ROPE_THETA ** (-jnp.arange(0, QK_ROPE, 2, dtype=F32) / QK_ROPE)
    ang = inv_freq[:, None] * positions.astype(F32).reshape(1, m)
    cos, sin = jnp.cos(ang), jnp.sin(ang)
    ones = jnp.ones((QK_NOPE, m), F32)
    zeros = jnp.zeros((QK_NOPE, m), F32)
    pad = jnp.zeros((HEAD_PAD - QK_NOPE - QK_ROPE, m), F32)
    cos_t = jnp.concatenate([ones, cos, cos, pad], axis=0)
    sin_t = jnp.concatenate([zeros, sin, sin, pad], axis=0)
    sin_q = jnp.concatenate([zeros, -sin, sin, pad], axis=0)
    return cos_t.T, sin_t.T, cos_t, sin_q


def kernel(x, positions, norm_mix_pre, w_in, mla_q_norm, w_uq, mla_kv_norm, w_ukv, w_o_mla, sc_conv_w, w_o_sc, cf_conv_w, cf_conv_b, cf_ln_g, cf_ln_b, w_o_cf, gm_ln_g, gm_ln_b, gm_ws, gm_bs, w_o_gm, gate_b, w_out, norm_mix_post, norm_ffn_pre, w_ffn_in, w_ffn_out, norm_ffn_post):
    p = dict(norm_mix_pre=norm_mix_pre, w_in=w_in, mla_q_norm=mla_q_norm, w_uq=w_uq,
             mla_kv_norm=mla_kv_norm, w_ukv=w_ukv, w_o_mla=w_o_mla, sc_conv_w=sc_conv_w,
             w_o_sc=w_o_sc, cf_conv_w=cf_conv_w, cf_conv_b=cf_conv_b, cf_ln_g=cf_ln_g,
             cf_ln_b=cf_ln_b, w_o_cf=w_o_cf, gm_ln_g=gm_ln_g, gm_ln_b=gm_ln_b, gm_ws=gm_ws,
             gm_bs=gm_bs, w_o_gm=w_o_gm, gate_b=gate_b, w_out=w_out, norm_mix_post=norm_mix_post,
             norm_ffn_pre=norm_ffn_pre, w_ffn_in=w_ffn_in, w_ffn_out=w_ffn_out,
             norm_ffn_post=norm_ffn_post)
    batch, seq, d = x.shape
    depth = w_in.shape[0]
    m = batch * seq
    assert d == D_MODEL and m % TM_PREP == 0 and m % TM_FFN == 0
    assert seq % QB_ATTN == 0 and seq % KC_ATTN == 0 and seq % TM_MIX == 0
    rope = _rope_tables(positions)
    x2 = x.reshape(m, d)
    sw = _stacked_weights(p)
    for l in range(depth):
        qt, k, vt = _prep(x2, sw, l, rope)
        att = _attention(qt, k, vt, batch, seq)
        x2 = _mix(x2, att, sw, l, batch, seq)
        x2 = _ffn(x2, sw, l)
    return x2.reshape(batch, seq, d)
```
